```python
import jax, jax.numpy as jnp
from jax import lax
import numpy as np

D_MODEL = 1024
BATCH = 2
SEQ = 8192
DEPTH = 2

CHUNK = 64
NORM_EPS = 1e-6
L2_EPS = 1e-6
CONV_WIDTH = D_MODEL // 2
CONV_K = 3
HGRN_HEADS = 4
HGRN_DK = 128
HGRN_DV = 128
HGRN_WIDTH = HGRN_HEADS * HGRN_DK
GDN_QK_HEADS = 4
GDN_V_HEADS = 8
GDN_DK = 128
GDN_DV = 128
GDN_CONV_K = 4
GDN_QK_WIDTH = GDN_QK_HEADS * GDN_DK
GDN_V_WIDTH = GDN_V_HEADS * GDN_DV
N_BRANCHES = 3
IN_SPLITS = (CONV_WIDTH, CONV_WIDTH, CONV_WIDTH, CONV_WIDTH,
             HGRN_WIDTH, HGRN_WIDTH, HGRN_HEADS * HGRN_DV, HGRN_HEADS * HGRN_DV,
             GDN_QK_WIDTH, GDN_QK_WIDTH, GDN_V_WIDTH, GDN_V_HEADS, GDN_V_HEADS, GDN_V_WIDTH,
             N_BRANCHES * D_MODEL)
IN_COLS = sum(IN_SPLITS)
MIN_F = 1e-30

kernel_name = "hybrid_conv_hgrn2_gdn_gated_merge"


def rmsnorm(x, w):
    xf = x.astype(jnp.float32)
    y = xf * lax.rsqrt(jnp.mean(xf * xf, axis=-1, keepdims=True) + NORM_EPS)
    return (y * w.astype(jnp.float32)).astype(x.dtype)


def head_rmsnorm(o, w):
    return o * lax.rsqrt(jnp.mean(o * o, axis=-1, keepdims=True) + NORM_EPS) * w.astype(jnp.float32)


def l2norm(x):
    return x * lax.rsqrt(jnp.sum(x * x, axis=-1, keepdims=True) + L2_EPS)


def masked_exp(diff, mask):
    return jnp.where(mask, jnp.exp(jnp.where(mask, diff, 0.0)), 0.0)


def causal_dwconv(x, w):
    k = w.shape[0]
    return lax.conv_general_dilated(
        x, w[:, None, :].astype(x.dtype), window_strides=(1,), padding=((k - 1, 0),),
        dimension_numbers=('NWC', 'WIO', 'NWC'), feature_group_count=x.shape[-1])


def to_chunks(x, heads):
    b, s, hd = x.shape
    return x.reshape(b, s // CHUNK, CHUNK, heads, hd // heads).transpose(0, 3, 1, 2, 4)


def scalar_chunks(x):
    b, s, h = x.shape
    return x.reshape(b, s // CHUNK, CHUNK, h).transpose(0, 3, 1, 2)


def from_chunks(o):
    b, h, n, l, d = o.shape
    return o.transpose(0, 2, 3, 1, 4).reshape(b, n * l, h, d)


def hgrn2_chunked(q, k, v, log_f):
    g = jnp.cumsum(log_f, axis=-2)
    g_last = g[..., -1:, :]
    q_inter = q * jnp.exp(g)
    k_state = k * jnp.exp(g_last - g)
    causal = jnp.tril(jnp.ones((CHUNK, CHUNK), dtype=bool))[:, :, None]
    mv = lambda a: jnp.moveaxis(a, 2, 0)

    def step(state, inp):
        qc, kc, vc, gc, qi, ks, gl = inp
        diff = gc[..., :, None, :] - gc[..., None, :, :]
        decay = masked_exp(diff, causal)
        scores = jnp.einsum('bhtk,bhtsk,bhsk->bhts', qc, decay, kc)
        o = jnp.einsum('bhts,bhsv->bhtv', scores, vc) + jnp.einsum('bhtk,bhkv->bhtv', qi, state)
        state = jnp.exp(gl)[..., 0, :, None] * state + jnp.einsum('bhsk,bhsv->bhkv', ks, vc)
        return state, o

    b, h = q.shape[0], q.shape[1]
    s0 = jnp.zeros((b, h, q.shape[-1], v.shape[-1]), jnp.float32)
    _, o = lax.scan(step, s0, (mv(q), mv(k), mv(v), mv(g), mv(q_inter), mv(k_state), mv(g_last)))
    return jnp.moveaxis(o, 0, 2)


def gated_delta_chunked(q, k, v, log_a, beta):
    g = jnp.cumsum(log_a, axis=-1)
    g_last = g[..., -1]
    diff = g[..., :, None] - g[..., None, :]
    strict = jnp.tril(jnp.ones((CHUNK, CHUNK), dtype=bool), -1)
    causal = jnp.tril(jnp.ones((CHUNK, CHUNK), dtype=bool))
    decay_strict = masked_exp(diff, strict)
    decay_causal = masked_exp(diff, causal)
    kb = k * beta[..., None]
    m = jnp.einsum('bhntk,bhnsk->bhnts', k, kb) * decay_strict
    eye = jnp.eye(CHUNK, dtype=jnp.float32)
    rhs = jnp.concatenate([v, k * jnp.exp(g)[..., None]], axis=-1)
    sol = lax.linalg.triangular_solve(eye + m, rhs, left_side=True, lower=True, unit_diagonal=True)
    u, w = sol[..., :v.shape[-1]], sol[..., v.shape[-1]:]
    a_qk = jnp.einsum('bhntk,bhnsk->bhnts', q, kb) * decay_causal
    q_inter = q * jnp.exp(g)[..., None]
    k_state = kb * jnp.exp(g_last[..., None] - g)[..., None]
    mv = lambda a: jnp.moveaxis(a, 2, 0)

    def step(state, inp):
        uc, wc, aqk, qi, ks, gl = inp
        e = uc - jnp.einsum('bhlk,bhkv->bhlv', wc, state)
        o = jnp.einsum('bhlk,bhkv->bhlv', qi, state) + jnp.einsum('bhts,bhsv->bhtv', aqk, e)
        state = jnp.exp(gl)[..., None, None] * state + jnp.einsum('bhlk,bhlv->bhkv', ks, e)
        return state, o

    b, h = q.shape[0], q.shape[1]
    s0 = jnp.zeros((b, h, q.shape[-1], v.shape[-1]), jnp.float32)
    _, o = lax.scan(step, s0, (mv(u), mv(w), mv(a_qk), mv(q_inter), mv(k_state), mv(g_last)))
    return jnp.moveaxis(o, 0, 2)


def setup_inputs(seed: int = 0) -> dict:
    key = jax.random.key(seed)
    ks = jax.random.split(key, 16)
    f32 = jnp.float32
    nrm = lambda k, shape, scale: jax.random.normal(k, shape, f32) * scale
    dt = jnp.exp(jax.random.uniform(ks[7], (DEPTH, GDN_V_HEADS), f32, np.log(1e-3), np.log(1e-1)))
    return {
        "x": nrm(ks[0], (BATCH, SEQ, D_MODEL), 1.0),
        "norm_w": 1.0 + nrm(ks[1], (DEPTH, D_MODEL), 0.02),
        "w_in": nrm(ks[2], (DEPTH, D_MODEL, IN_COLS), D_MODEL ** -0.5),
        "b_gate": nrm(ks[3], (DEPTH, N_BRANCHES * D_MODEL), 0.1),
        "conv_a": nrm(ks[4], (DEPTH, CONV_K, CONV_WIDTH), CONV_K ** -0.5),
        "conv_c": nrm(ks[5], (DEPTH, GDN_CONV_K, 2 * GDN_QK_WIDTH + GDN_V_WIDTH), GDN_CONV_K ** -0.5),
        "a_log": jnp.log(jax.random.uniform(ks[6], (DEPTH, GDN_V_HEADS), f32, 1.0, 16.0)),
        "dt_bias": dt + jnp.log(-jnp.expm1(-dt)),
        "lower_bounds": nrm(ks[8], (DEPTH, HGRN_WIDTH), 0.5),
        "hgrn_norm_w": 1.0 + nrm(ks[9], (DEPTH, HGRN_DV), 0.02),
        "gdn_norm_w": 1.0 + nrm(ks[10], (DEPTH, GDN_DV), 0.02),
        "w_out_a": nrm(ks[11], (DEPTH, CONV_WIDTH, D_MODEL), CONV_WIDTH ** -0.5),
        "w_out_b": nrm(ks[12], (DEPTH, HGRN_HEADS * HGRN_DV, D_MODEL), (HGRN_HEADS * HGRN_DV) ** -0.5),
        "w_out_c": nrm(ks[13], (DEPTH, GDN_V_WIDTH, D_MODEL), GDN_V_WIDTH ** -0.5),
        "w_o": nrm(ks[14], (DEPTH, D_MODEL, D_MODEL), D_MODEL ** -0.5),
        "final_norm_w": 1.0 + nrm(ks[15], (D_MODEL,), 0.02),
    }


def reference(x, norm_w, w_in, b_gate, conv_a, conv_c, a_log, dt_bias, lower_bounds,
              hgrn_norm_w, gdn_norm_w, w_out_a, w_out_b, w_out_c, w_o, final_norm_w):
    f32 = jnp.float32
    bsz, seq = x.shape[0], x.shape[1]
    split_idx = tuple(int(s) for s in np.cumsum(IN_SPLITS)[:-1])
    p = jax.nn.softmax(lower_bounds.astype(f32), axis=0)
    lbs = jnp.cumsum(p, axis=0) - p[0]

    for l in range(DEPTH):
        h = rmsnorm(x, norm_w[l])
        proj = jnp.einsum('bsd,dc->bsc', h, w_in[l])
        (a_b, a_c, a_x, a_z, b_q, b_f, b_i, b_z,
         c_q, c_k, c_v, c_beta, c_a, c_z, g_all) = jnp.split(proj, split_idx, axis=-1)
        gates = jax.nn.sigmoid(g_all + b_gate[l]).reshape(bsz, seq, N_BRANCHES, D_MODEL)

        y_a = a_b * causal_dwconv(a_c * a_x, conv_a[l]) * jax.nn.silu(a_z)
        y_a = jnp.einsum('bsc,cd->bsd', y_a, w_out_a[l])

        lb = lbs[l].reshape(1, HGRN_HEADS, 1, 1, HGRN_DK)
        qb = to_chunks(jax.nn.silu(b_q.astype(f32)) * HGRN_DK ** -0.5, HGRN_HEADS)
        fz = to_chunks(b_f.astype(f32), HGRN_HEADS)
        f_gate = lb + (1.0 - lb) * jax.nn.sigmoid(fz)
        log_f = jnp.log(jnp.maximum(f_gate, MIN_F))
        kb = 1.0 - f_gate
        vb = to_chunks(b_i.astype(f32), HGRN_HEADS)
        o_b = from_chunks(hgrn2_chunked(qb, kb, vb, log_f))
        y_b = head_rmsnorm(o_b, hgrn_norm_w[l]).reshape(bsz, seq, -1) * jax.nn.silu(b_z.astype(f32))
        y_b = jnp.einsum('bsc,cd->bsd', y_b.astype(x.dtype), w_out_b[l])

        qkv = jax.nn.silu(causal_dwconv(jnp.concatenate([c_q, c_k, c_v], axis=-1), conv_c[l])).astype(f32)
        qc, kc, vc = jnp.split(qkv, (GDN_QK_WIDTH, 2 * GDN_QK_WIDTH), axis=-1)
        rep = GDN_V_HEADS // GDN_QK_HEADS
        qc = jnp.repeat(l2norm(qc.reshape(bsz, seq, GDN_QK_HEADS, GDN_DK)), rep, axis=2) * GDN_DK ** -0.5
        kc = jnp.repeat(l2norm(kc.reshape(bsz, seq, GDN_QK_HEADS, GDN_DK)), rep, axis=2)
        beta = jax.nn.sigmoid(c_beta.astype(f32))
        log_a = -jnp.exp(a_log[l].astype(f32)) * jax.nn.softplus(c_a.astype(f32) + dt_bias[l].astype(f32))
        o_c = gated_delta_chunked(
            to_chunks(qc.reshape(bsz, seq, -1), GDN_V_HEADS), to_chunks(kc.reshape(bsz, seq, -1), GDN_V_HEADS),
            to_chunks(vc, GDN_V_HEADS), scalar_chunks(log_a), scalar_chunks(beta))
        y_c = head_rmsnorm(from_chunks(o_c), gdn_norm_w[l]).reshape(bsz, seq, -1) * jax.nn.silu(c_z.astype(f32))
        y_c = jnp.einsum('bsc,cd->bsd', y_c.astype(x.dtype), w_out_c[l])

        merged = gates[:, :, 0] * y_a + gates[:, :, 1] * y_b + gates[:, :, 2] * y_c
        x = x + jnp.einsum('bsd,de->bse', merged, w_o[l])

    return rmsnorm(x, final_norm_w)
```

```python
import functools

import jax
import jax.numpy as jnp
from jax import lax
from jax.experimental import pallas as pl
from jax.experimental.pallas import tpu as pltpu

F32 = jnp.float32
BF16 = jnp.bfloat16

D_MODEL = 1024
DEPTH = 2
CHUNK = 64
NORM_EPS = 1e-6
L2_EPS = 1e-6
MIN_F = 1e-30
CONV_WIDTH = 512
CONV_K = 3
HGRN_HEADS = 4
HGRN_DK = 128
HGRN_WIDTH = HGRN_HEADS * HGRN_DK
GDN_QK_HEADS = 4
GDN_V_HEADS = 8
GDN_DK = 128
GDN_DV = 128
GDN_CONV_K = 4
GDN_QK_WIDTH = GDN_QK_HEADS * GDN_DK
GDN_V_WIDTH = GDN_V_HEADS * GDN_DV

COL_A = 0
COL_B = 2048
COL_CQ = 4096
COL_CV = 5120
COL_CZ = 6144
COL_G = 7168
MAIN_COLS = 10240
SMALL_SRC = 6144
SMALL_COLS = 128

V7X_SUBLANES = 8
V7X_LANES = 128
V7X_VMEM_BYTES = 64 * 1024 * 1024


def _vmem_limit(block_bytes, scratch_bytes=0):
    need = 2 * block_bytes + scratch_bytes
    return int(min(V7X_VMEM_BYTES - 8 * 1024 * 1024, need + need // 2 + 8 * 1024 * 1024))


def _sigmoid(x):
    return 1.0 / (1.0 + jnp.exp(-x))


def _silu(x):
    return x * _sigmoid(x)


def _softplus(x):
    return jnp.maximum(x, 0.0) + jnp.log(1.0 + jnp.exp(-jnp.abs(x)))


def _dot(a, b):
    return jnp.dot(a.astype(BF16), b.astype(BF16), preferred_element_type=F32)


def _dot_nt(a, b):
    return lax.dot_general(a.astype(BF16), b.astype(BF16), (((1,), (1,)), ((), ())),
                           preferred_element_type=F32)


def _dot_hi(a, b):
    return jnp.dot(a, b, preferred_element_type=F32, precision=lax.Precision.HIGHEST)


def _shift_rows(x, carry, j):
    xr = pltpu.roll(x, j, 0)
    cr = pltpu.roll(carry, j, 0)
    row = lax.broadcasted_iota(jnp.int32, carry.shape, 0)
    top = jnp.where(row < j, cr, xr[0:V7X_SUBLANES])
    return jnp.concatenate([top, xr[V7X_SUBLANES:]], axis=0)


def _chunk_cumsum_rows(x):
    pos = lax.broadcasted_iota(jnp.int32, x.shape, 0) & (CHUNK - 1)
    d = 1
    while d < CHUNK:
        xr = pltpu.roll(x, d, 0)
        x = x + jnp.where(pos >= d, xr, 0.0)
        d *= 2
    return x


def _head_rmsnorm(o, w):
    return o * lax.rsqrt(jnp.mean(o * o, axis=-1, keepdims=True) + NORM_EPS) * w


def _inproj_kernel(x_ref, nw_ref, w_ref, ws_ref, proj_ref, small_ref, h_ref):
    @pl.when(pl.program_id(1) == 0)
    def _():
        x = x_ref[...]
        ms = jnp.mean(x * x, axis=-1, keepdims=True)
        h = (x * lax.rsqrt(ms + NORM_EPS) * nw_ref[...]).astype(BF16)
        h_ref[...] = h
        small_ref[...] = jnp.dot(h, ws_ref[...], preferred_element_type=F32)

    proj_ref[...] = jnp.dot(h_ref[...], w_ref[...], preferred_element_type=F32)


def _inproj(x2, norm_w, w_main, w_small, *, tm=1024, tn=1024):
    t = x2.shape[0]
    block_bytes = tm * D_MODEL * 4 + D_MODEL * tn * 2 + tm * tn * 4 + tm * SMALL_COLS * 4 + D_MODEL * SMALL_COLS * 2
    return pl.pallas_call(
        _inproj_kernel,
        grid=(t // tm, MAIN_COLS // tn),
        in_specs=[
            pl.BlockSpec((tm, D_MODEL), lambda i, j: (i, 0)),
            pl.BlockSpec((1, D_MODEL), lambda i, j: (0, 0)),
            pl.BlockSpec((D_MODEL, tn), lambda i, j: (0, j)),
            pl.BlockSpec((D_MODEL, SMALL_COLS), lambda i, j: (0, 0)),
        ],
        out_specs=[
            pl.BlockSpec((tm, tn), lambda i, j: (i, j)),
            pl.BlockSpec((tm, SMALL_COLS), lambda i, j: (i, 0)),
        ],
        out_shape=[
            jax.ShapeDtypeStruct((t, MAIN_COLS), F32),
            jax.ShapeDtypeStruct((t, SMALL_COLS), F32),
        ],
        scratch_shapes=[pltpu.VMEM((tm, D_MODEL), BF16)],
        compiler_params=pltpu.CompilerParams(
            dimension_semantics=("arbitrary", "arbitrary"),
            vmem_limit_bytes=_vmem_limit(block_bytes, tm * D_MODEL * 2)),
        name="inproj",
    )(x2, norm_w, w_main, w_small)


def _hgrn_chunk(q, k, v, g, st):
    n = CHUNK
    rowi = lax.broadcasted_iota(jnp.int32, (n, n), 0)
    coli = lax.broadcasted_iota(jnp.int32, (n, n), 1)
    a = jnp.zeros((n, n), F32)
    for half in (32, 16, 8):
        blk = 2 * half
        nb = n // blk
        gref = jnp.broadcast_to(g.reshape(nb, blk, HGRN_DK)[:, half:half + 1, :],
                                (nb, blk, HGRN_DK)).reshape(n, HGRN_DK)
        qt = q * jnp.exp(jnp.minimum(g - gref, 0.0))
        kt = k * jnp.exp(jnp.minimum(gref - g, 0.0))
        s = _dot_nt(qt, kt)
        mask = ((rowi & -blk) == (coli & -blk)) & ((rowi & (blk - 1)) >= half) & ((coli & (blk - 1)) < half)
        a = a + jnp.where(mask, s, 0.0)
    g3 = g.reshape(n // 8, 8, HGRN_DK)
    k3 = k.reshape(n // 8, 8, HGRN_DK)
    for j in range(8):
        gs = jnp.broadcast_to(g3[:, j:j + 1, :], g3.shape).reshape(n, HGRN_DK)
        ks = jnp.broadcast_to(k3[:, j:j + 1, :], k3.shape).reshape(n, HGRN_DK)
        dec = jnp.exp(jnp.minimum(g - gs, 0.0))
        col = jnp.sum(q * dec * ks, axis=-1, keepdims=True)
        mask = (coli == (rowi & -8) + j) & ((rowi & 7) >= j)
        a = a + jnp.where(mask, col, 0.0)
    g_last = g[n - 1:n, :]
    o = _dot(a, v) + _dot_nt(q * jnp.exp(g), st)
    k_state = k * jnp.exp(g_last - g)
    st_new = jnp.exp(g_last) * st + _dot(v.T, k_state)
    return o, st_new


def _hgrn_kernel(q_ref, f_ref, i_ref, z_ref, lb_ref, nw_ref, y_ref, st_ref, *, layer, tb):
    @pl.when(pl.program_id(1) == 0)
    def _():
        st_ref[...] = jnp.zeros_like(st_ref)

    rows = [lb_ref[i:i + 1, :] for i in range(DEPTH)]
    mx = functools.reduce(jnp.maximum, rows)
    es = [jnp.exp(r - mx) for r in rows]
    tot = functools.reduce(lambda a_, b_: a_ + b_, es)
    lb = jnp.zeros_like(tot)
    for i in range(1, layer + 1):
        lb = lb + es[i] / tot

    f_gate = lb + (1.0 - lb) * _sigmoid(f_ref[...])
    log_f = jnp.log(jnp.maximum(f_gate, MIN_F))
    kk = 1.0 - f_gate
    qq = _silu(q_ref[...]) * (HGRN_DK ** -0.5)
    g = _chunk_cumsum_rows(log_f)
    vv = i_ref[...]
    zz = z_ref[...]
    nw = nw_ref[...]
    for h in range(HGRN_HEADS):
        cs = slice(h * HGRN_DK, (h + 1) * HGRN_DK)
        st = st_ref[h]
        for c in range(tb // CHUNK):
            rs = slice(c * CHUNK, (c + 1) * CHUNK)
            o, st = _hgrn_chunk(qq[rs, cs], kk[rs, cs], vv[rs, cs], g[rs, cs], st)
            y_ref[rs, cs] = _head_rmsnorm(o, nw) * _silu(zz[rs, cs])
        st_ref[h] = st


def _hgrn(proj, lower_bounds, norm_w, *, layer, bsz, seq, tb=128):
    nblk = seq // tb
    cb = COL_B // HGRN_WIDTH
    spec = lambda off: pl.BlockSpec((tb, HGRN_WIDTH), lambda b, s, off=off: (b * nblk + s, cb + off))
    block_bytes = 5 * tb * HGRN_WIDTH * 4
    return pl.pallas_call(
        functools.partial(_hgrn_kernel, layer=layer, tb=tb),
        grid=(bsz, nblk),
        in_specs=[spec(0), spec(1), spec(2), spec(3),
                  pl.BlockSpec((DEPTH, HGRN_WIDTH), lambda b, s: (0, 0)),
                  pl.BlockSpec((1, HGRN_DK), lambda b, s: (0, 0))],
        out_specs=pl.BlockSpec((tb, HGRN_WIDTH), lambda b, s: (b * nblk + s, 0)),
        out_shape=jax.ShapeDtypeStruct((bsz * seq, HGRN_WIDTH), F32),
        scratch_shapes=[pltpu.VMEM((HGRN_HEADS, HGRN_DK, HGRN_DK), F32)],
        compiler_params=pltpu.CompilerParams(
            dimension_semantics=("arbitrary", "arbitrary"),
            vmem_limit_bytes=_vmem_limit(block_bytes, HGRN_HEADS * HGRN_DK * HGRN_DK * 4)),
        name="hgrn",
    )(proj, proj, proj, proj, lower_bounds, norm_w)


def _conv4_silu(x, carry_ref, w):
    carry = carry_ref[...]
    n = x.shape[0]
    acc = x * w[GDN_CONV_K - 1:GDN_CONV_K, :]
    for j in range(1, GDN_CONV_K):
        acc = acc + _shift_rows(x, carry, j) * w[GDN_CONV_K - 1 - j:GDN_CONV_K - j, :]
    carry_ref[...] = x[n - V7X_SUBLANES:n, :]
    return _silu(acc)


def _l2norm_heads(x, heads, width):
    outs = []
    for h in range(heads):
        xh = x[:, h * width:(h + 1) * width]
        outs.append(xh * lax.rsqrt(jnp.sum(xh * xh, axis=-1, keepdims=True) + L2_EPS))
    return outs


def _unit_lower_inverse(m):
    n = CHUNK
    rowi = lax.broadcasted_iota(jnp.int32, (n, n), 0)
    coli = lax.broadcasted_iota(jnp.int32, (n, n), 1)
    eye = jnp.where(rowi == coli, 1.0, 0.0).astype(F32)
    p = -m
    s = eye + p
    pk = _dot_hi(p, p)
    k = 2
    while k < n // 2:
        r = _dot_hi(pk, jnp.concatenate([pk, s], axis=1))
        pk = r[:, :n]
        s = s + r[:, n:]
        k *= 2
    return s + _dot_hi(pk, s)


def _gdn_kernel(q_ref, k_ref, v_ref, z_ref, sm_ref, cw_ref, alog_ref, dtb_ref, nw_ref,
                y_ref, st_ref, cq_ref, ck_ref, cv_ref, *, tb):
    @pl.when(pl.program_id(1) == 0)
    def _():
        st_ref[...] = jnp.zeros_like(st_ref)
        cq_ref[...] = jnp.zeros_like(cq_ref)
        ck_ref[...] = jnp.zeros_like(ck_ref)
        cv_ref[...] = jnp.zeros_like(cv_ref)

    cw = cw_ref[...]
    qc = _conv4_silu(q_ref[...], cq_ref, cw[:, 0:GDN_QK_WIDTH])
    kc = _conv4_silu(k_ref[...], ck_ref, cw[:, GDN_QK_WIDTH:2 * GDN_QK_WIDTH])
    vc = _conv4_silu(v_ref[...], cv_ref, cw[:, 2 * GDN_QK_WIDTH:])
    qs = [x * (GDN_DK ** -0.5) for x in _l2norm_heads(qc, GDN_QK_HEADS, GDN_DK)]
    ks = _l2norm_heads(kc, GDN_QK_HEADS, GDN_DK)

    sm = sm_ref[...]
    lane = lax.broadcasted_iota(jnp.int32, sm.shape, 1)
    beta_all = _sigmoid(sm)
    log_a = -jnp.exp(alog_ref[...]) * _softplus(sm + dtb_ref[...])
    g_all = _chunk_cumsum_rows(log_a)
    comb = jnp.where(lane < GDN_V_HEADS, beta_all, g_all)
    comb_t = comb.T

    zz = z_ref[...]
    nw = nw_ref[...]
    n = CHUNK
    rowi = lax.broadcasted_iota(jnp.int32, (n, n), 0)
    coli = lax.broadcasted_iota(jnp.int32, (n, n), 1)
    causal = rowi >= coli
    strict = rowi > coli
    rep = GDN_V_HEADS // GDN_QK_HEADS
    for c in range(tb // CHUNK):
        rs = slice(c * CHUNK, (c + 1) * CHUNK)
        for hq in range(GDN_QK_HEADS):
            qh = qs[hq][rs]
            kh = ks[hq][rs]
            kk_t = _dot_nt(kh, kh)
            qk_t = _dot_nt(qh, kh)
            for r in range(rep):
                vh = hq * rep + r
                vs = slice(vh * GDN_DV, (vh + 1) * GDN_DV)
                g_col = g_all[rs, GDN_V_HEADS + vh:GDN_V_HEADS + vh + 1]
                g_row = comb_t[GDN_V_HEADS + vh:GDN_V_HEADS + vh + 1, rs]
                b_col = beta_all[rs, vh:vh + 1]
                b_row = comb_t[vh:vh + 1, rs]
                decay = jnp.exp(jnp.minimum(g_col - g_row, 0.0)) * b_row
                m = jnp.where(strict, kk_t * decay, 0.0)
                a_qk = jnp.where(causal, qk_t * decay, 0.0)
                t_inv = _unit_lower_inverse(m)
                eg = jnp.exp(g_col)
                rhs = jnp.concatenate([vc[rs, vs], kh * eg], axis=1)
                sol = _dot_hi(t_inv, rhs)
                u = sol[:, :GDN_DV]
                w = sol[:, GDN_DV:]
                g_last = g_col[n - 1:n, :]
                st = st_ref[vh]
                e = u - _dot(w, st)
                o = _dot(qh * eg, st) + _dot(a_qk, e)
                k_state = kh * (b_col * jnp.exp(g_last - g_col))
                st_ref[vh] = jnp.exp(g_last) * st + _dot(k_state.T, e)
                y_ref[rs, vs] = _head_rmsnorm(o, nw) * _silu(zz[rs, vs])


def _gdn(proj, small, conv_c, alog_pad, dtb_pad, norm_w, *, bsz, seq, tb=128):
    nblk = seq // tb
    row = lambda b, s: b * nblk + s
    block_bytes = (2 * tb * GDN_QK_WIDTH + 3 * tb * GDN_V_WIDTH + tb * SMALL_COLS) * 4 + GDN_CONV_K * 2048 * 4
    scratch_bytes = (GDN_V_HEADS * GDN_DK * GDN_DV + V7X_SUBLANES * 2048) * 4
    return pl.pallas_call(
        functools.partial(_gdn_kernel, tb=tb),
        grid=(bsz, nblk),
        in_specs=[
            pl.BlockSpec((tb, GDN_QK_WIDTH), lambda b, s: (row(b, s), COL_CQ // GDN_QK_WIDTH)),
            pl.BlockSpec((tb, GDN_QK_WIDTH), lambda b, s: (row(b, s), COL_CQ // GDN_QK_WIDTH + 1)),
            pl.BlockSpec((tb, GDN_V_WIDTH), lambda b, s: (row(b, s), COL_CV // GDN_V_WIDTH)),
            pl.BlockSpec((tb, GDN_V_WIDTH), lambda b, s: (row(b, s), COL_CZ // GDN_V_WIDTH)),
            pl.BlockSpec((tb, SMALL_COLS), lambda b, s: (row(b, s), 0)),
            pl.BlockSpec((GDN_CONV_K, 2 * GDN_QK_WIDTH + GDN_V_WIDTH), lambda b, s: (0, 0)),
            pl.BlockSpec((1, SMALL_COLS), lambda b, s: (0, 0)),
            pl.BlockSpec((1, SMALL_COLS), lambda b, s: (0, 0)),
            pl.BlockSpec((1, GDN_DV), lambda b, s: (0, 0)),
        ],
        out_specs=pl.BlockSpec((tb, GDN_V_WIDTH), lambda b, s: (row(b, s), 0)),
        out_shape=jax.ShapeDtypeStruct((bsz * seq, GDN_V_WIDTH), F32),
        scratch_shapes=[
            pltpu.VMEM((GDN_V_HEADS, GDN_DK, GDN_DV), F32),
            pltpu.VMEM((V7X_SUBLANES, GDN_QK_WIDTH), F32),
            pltpu.VMEM((V7X_SUBLANES, GDN_QK_WIDTH), F32),
            pltpu.VMEM((V7X_SUBLANES, GDN_V_WIDTH), F32),
        ],
        compiler_params=pltpu.CompilerParams(
            dimension_semantics=("arbitrary", "arbitrary"),
            vmem_limit_bytes=_vmem_limit(block_bytes, scratch_bytes)),
        name="gdn",
    )(proj, proj, proj, proj, small, conv_c, alog_pad, dtb_pad, norm_w)


def _merge_kernel(x_ref, ab_ref, ac_ref, ax_ref, az_ref, g0_ref, g1_ref, g2_ref, yb_ref, yc_ref,
                  ca_ref, bg_ref, wa_ref, wb_ref, wc_ref, wo_ref, fw_ref, out_ref, carry_ref, *, final):
    @pl.when(pl.program_id(1) == 0)
    def _():
        carry_ref[...] = jnp.zeros_like(carry_ref)

    u = ac_ref[...] * ax_ref[...]
    n = u.shape[0]
    carry = carry_ref[...]
    ca = ca_ref[...]
    conv = u * ca[CONV_K - 1:CONV_K, :]
    for j in range(1, CONV_K):
        conv = conv + _shift_rows(u, carry, j) * ca[CONV_K - 1 - j:CONV_K - j, :]
    carry_ref[...] = u[n - V7X_SUBLANES:n, :]
    ya = ab_ref[...] * conv * _silu(az_ref[...])

    ya2 = jnp.dot(ya.astype(BF16), wa_ref[...], preferred_element_type=F32)
    yb2 = jnp.dot(yb_ref[...].astype(BF16), wb_ref[...], preferred_element_type=F32)
    yc2 = jnp.dot(yc_ref[...].astype(BF16), wc_ref[...], preferred_element_type=F32)
    bg = bg_ref[...]
    merged = (_sigmoid(g0_ref[...] + bg[:, 0:D_MODEL]) * ya2
              + _sigmoid(g1_ref[...] + bg[:, D_MODEL:2 * D_MODEL]) * yb2
              + _sigmoid(g2_ref[...] + bg[:, 2 * D_MODEL:]) * yc2)
    out = x_ref[...] + jnp.dot(merged.astype(BF16), wo_ref[...], preferred_element_type=F32)
    if final:
        ms = jnp.mean(out * out, axis=-1, keepdims=True)
        out = out * lax.rsqrt(ms + NORM_EPS) * fw_ref[...]
    out_ref[...] = out


def _merge(x2, proj, yb, yc, conv_a, b_gate, wa, wb, wc, wo, final_w, *, final, bsz, seq, tm=256):
    nblk = seq // tm
    row = lambda b, s: b * nblk + s
    a_spec = lambda off: pl.BlockSpec((tm, CONV_WIDTH), lambda b, s, off=off: (row(b, s), COL_A // CONV_WIDTH + off))
    g_spec = lambda off: pl.BlockSpec((tm, D_MODEL), lambda b, s, off=off: (row(b, s), COL_G // D_MODEL + off))
    full = lambda shape: pl.BlockSpec(shape, lambda b, s: (0, 0))
    block_bytes = (tm * D_MODEL * 4 * 6 + tm * CONV_WIDTH * 4 * 5
                   + (2 * CONV_WIDTH + 2 * D_MODEL) * D_MODEL * 2 + 8 * D_MODEL * 4)
    return pl.pallas_call(
        functools.partial(_merge_kernel, final=final),
        grid=(bsz, nblk),
        in_specs=[
            pl.BlockSpec((tm, D_MODEL), lambda b, s: (row(b, s), 0)),
            a_spec(0), a_spec(1), a_spec(2), a_spec(3),
            g_spec(0), g_spec(1), g_spec(2),
            pl.BlockSpec((tm, HGRN_WIDTH), lambda b, s: (row(b, s), 0)),
            pl.BlockSpec((tm, GDN_V_WIDTH), lambda b, s: (row(b, s), 0)),
            full((CONV_K, CONV_WIDTH)),
            full((1, 3 * D_MODEL)),
            full((CONV_WIDTH, D_MODEL)),
            full((HGRN_WIDTH, D_MODEL)),
            full((GDN_V_WIDTH, D_MODEL)),
            full((D_MODEL, D_MODEL)),
            full((1, D_MODEL)),
        ],
        out_specs=pl.BlockSpec((tm, D_MODEL), lambda b, s: (row(b, s), 0)),
        out_shape=jax.ShapeDtypeStruct((bsz * seq, D_MODEL), F32),
        scratch_shapes=[pltpu.VMEM((V7X_SUBLANES, CONV_WIDTH), F32)],
        compiler_params=pltpu.CompilerParams(
            dimension_semantics=("arbitrary", "arbitrary"),
            vmem_limit_bytes=_vmem_limit(block_bytes, V7X_SUBLANES * CONV_WIDTH * 4)),
        name="merge_final" if final else "merge",
    )(x2, proj, proj, proj, proj, proj, proj, proj, yb, yc, conv_a, b_gate, wa, wb, wc, wo, final_w)


def kernel(x, norm_w, w_in, b_gate, conv_a, conv_c, a_log, dt_bias, lower_bounds, hgrn_norm_w, gdn_norm_w,
           w_out_a, w_out_b, w_out_c, w_o, final_norm_w):
    bsz, seq, _ = x.shape
    x2 = x.reshape(bsz * seq, D_MODEL)
    pad8 = lambda a: jnp.pad(a.astype(F32)[None, :], ((0, 0), (GDN_V_HEADS, SMALL_COLS - 2 * GDN_V_HEADS)))
    for l in range(DEPTH):
        w_l = w_in[l]
        w_main = jnp.concatenate([w_l[:, :SMALL_SRC], w_l[:, SMALL_SRC + 2 * GDN_V_HEADS:]], axis=1).astype(BF16)
        w_small = jnp.pad(w_l[:, SMALL_SRC:SMALL_SRC + 2 * GDN_V_HEADS],
                          ((0, 0), (0, SMALL_COLS - 2 * GDN_V_HEADS))).astype(BF16)
        proj, small = _inproj(x2, norm_w[l][None, :], w_main, w_small)
        yb = _hgrn(proj, lower_bounds, hgrn_norm_w[l][None, :], layer=l, bsz=bsz, seq=seq)
        yc = _gdn(proj, small, conv_c[l], pad8(a_log[l]), pad8(dt_bias[l]), gdn_norm_w[l][None, :],
                  bsz=bsz, seq=seq)
        x2 = _merge(x2, proj, yb, yc, conv_a[l], b_gate[l][None, :],
                    w_out_a[l].astype(BF16), w_out_b[l].astype(BF16), w_out_c[l].astype(BF16),
                    w_o[l].astype(BF16), final_norm_w[None, :],
                    final=(l == DEPTH - 1), bsz=bsz, seq=seq)
    return x2.reshape(bsz, seq, D_MODEL)
```

```python
import functools

import jax
import jax.numpy as jnp
from jax import lax
from jax.experimental import pallas as pl
from jax.experimental.pallas import tpu as pltpu

F32 = jnp.float32
BF16 = jnp.bfloat16

D_MODEL = 1024
DEPTH = 2
CHUNK = 64
NORM_EPS = 1e-6
L2_EPS = 1e-6
MIN_F = 1e-30
CONV_WIDTH = 512
CONV_K = 3
HGRN_HEADS = 4
HGRN_DK = 128
HGRN_WIDTH = HGRN_HEADS * HGRN_DK
GDN_QK_HEADS = 4
GDN_V_HEADS = 8
GDN_DK = 128
GDN_DV = 128
GDN_CONV_K = 4
GDN_QK_WIDTH = GDN_QK_HEADS * GDN_DK
GDN_V_WIDTH = GDN_V_HEADS * GDN_DV
GDN_CONV_WIDTH = 2 * GDN_QK_WIDTH + GDN_V_WIDTH

SRC_A = 0
SRC_B = SRC_A + 4 * CONV_WIDTH
SRC_CQKV = SRC_B + 4 * HGRN_WIDTH
SRC_SMALL = SRC_CQKV + GDN_CONV_WIDTH
SRC_CZ = SRC_SMALL + 2 * GDN_V_HEADS
SRC_G = SRC_CZ + GDN_V_WIDTH
SMALL_COLS = 128

WB_COLS = 4 * HGRN_WIDTH
WC_Q, WC_K, WC_V, WC_Z, WC_SMALL = 0, 512, 1024, 2048, 3072
WC_COLS = WC_SMALL + SMALL_COLS
WM_A, WM_G = 0, 4 * CONV_WIDTH
WM_COLS = WM_G + 3 * D_MODEL

V7X_SUBLANES = 8
V7X_VMEM_BYTES = 64 * 1024 * 1024


def _vmem_limit(block_bytes, resident_bytes, temp_bytes):
    need = 2 * block_bytes + resident_bytes + temp_bytes
    return int(min(V7X_VMEM_BYTES - 8 * 1024 * 1024, need + need // 4 + 4 * 1024 * 1024))


def _resident(shape):
    return pl.BlockSpec(shape, lambda b, s: (0,) * len(shape), pipeline_mode=pl.Buffered(1))


def _sigmoid(x):
    return 1.0 / (1.0 + jnp.exp(-x))


def _silu(x):
    return x * _sigmoid(x)


def _softplus(x):
    return jnp.maximum(x, 0.0) + jnp.log(1.0 + jnp.exp(-jnp.abs(x)))


def _dot(a, b):
    return jnp.dot(a.astype(BF16), b.astype(BF16), preferred_element_type=F32)


def _dot_nt(a, b):
    return lax.dot_general(a.astype(BF16), b.astype(BF16), (((1,), (1,)), ((), ())),
                           preferred_element_type=F32)


def _rmsnorm_bf16(x, w):
    ms = jnp.mean(x * x, axis=-1, keepdims=True)
    return (x * lax.rsqrt(ms + NORM_EPS) * w).astype(BF16)


def _proj(h, w_ref, start, width):
    return jnp.dot(h, w_ref[:, start:start + width], preferred_element_type=F32)


def _causal_conv(x, pad_ref, w, width):
    n = x.shape[0]
    pad_ref[V7X_SUBLANES:V7X_SUBLANES + n, :] = x
    acc = x * w[width - 1:width, :]
    for j in range(1, width):
        acc = acc + pad_ref[V7X_SUBLANES - j:V7X_SUBLANES - j + n, :] * w[width - 1 - j:width - j, :]
    pad_ref[0:V7X_SUBLANES, :] = x[n - V7X_SUBLANES:n, :]
    return acc


def _chunk_cumsum_rows(x):
    pos = lax.broadcasted_iota(jnp.int32, x.shape, 0) & (CHUNK - 1)
    d = 1
    while d < CHUNK:
        xr = pltpu.roll(x, d, 0)
        x = x + jnp.where(pos >= d, xr, 0.0)
        d *= 2
    return x


def _head_rmsnorm(o, w):
    return o * lax.rsqrt(jnp.mean(o * o, axis=-1, keepdims=True) + NORM_EPS) * w


def _hgrn_chunk(q, k, v, g, st):
    n = CHUNK
    rowi = lax.broadcasted_iota(jnp.int32, (n, n), 0)
    coli = lax.broadcasted_iota(jnp.int32, (n, n), 1)
    a = jnp.zeros((n, n), F32)
    for half in (32, 16, 8):
        blk = 2 * half
        nb = n // blk
        gref = jnp.broadcast_to(g.reshape(nb, blk, HGRN_DK)[:, half:half + 1, :],
                                (nb, blk, HGRN_DK)).reshape(n, HGRN_DK)
        qt = q * jnp.exp(jnp.minimum(g - gref, 0.0))
        kt = k * jnp.exp(jnp.minimum(gref - g, 0.0))
        s = _dot_nt(qt, kt)
        mask = ((rowi & -blk) == (coli & -blk)) & ((rowi & (blk - 1)) >= half) & ((coli & (blk - 1)) < half)
        a = a + jnp.where(mask, s, 0.0)
    g3 = g.reshape(n // 8, 8, HGRN_DK)
    k3 = k.reshape(n // 8, 8, HGRN_DK)
    for j in range(8):
        gs = jnp.broadcast_to(g3[:, j:j + 1, :], g3.shape).reshape(n, HGRN_DK)
        ks = jnp.broadcast_to(k3[:, j:j + 1, :], k3.shape).reshape(n, HGRN_DK)
        dec = jnp.exp(jnp.minimum(g - gs, 0.0))
        col = jnp.sum(q * dec * ks, axis=-1, keepdims=True)
        mask = (coli == (rowi & -8) + j) & ((rowi & 7) >= j)
        a = a + jnp.where(mask, col, 0.0)
    g_last = g[n - 1:n, :]
    o = _dot(a, v) + _dot_nt(q * jnp.exp(g), st)
    k_state = k * jnp.exp(g_last - g)
    st_new = jnp.exp(g_last) * st + _dot(v.T, k_state)
    return o, st_new


def _hgrn_kernel(x_ref, nw_ref, w_ref, lb_ref, hw_ref, y_ref, st_ref, *, layer, tb):
    @pl.when(pl.program_id(1) == 0)
    def _():
        st_ref[...] = jnp.zeros_like(st_ref)

    h = _rmsnorm_bf16(x_ref[...], nw_ref[...])
    rows = [lb_ref[i:i + 1, :] for i in range(DEPTH)]
    mx = functools.reduce(jnp.maximum, rows)
    es = [jnp.exp(r - mx) for r in rows]
    tot = functools.reduce(lambda a_, b_: a_ + b_, es)
    lb = jnp.zeros_like(tot)
    for i in range(1, layer + 1):
        lb = lb + es[i] / tot

    f_gate = lb + (1.0 - lb) * _sigmoid(_proj(h, w_ref, HGRN_WIDTH, HGRN_WIDTH))
    log_f = jnp.log(jnp.maximum(f_gate, MIN_F))
    kk = 1.0 - f_gate
    qq = _silu(_proj(h, w_ref, 0, HGRN_WIDTH)) * (HGRN_DK ** -0.5)
    g = _chunk_cumsum_rows(log_f)
    vv = _proj(h, w_ref, 2 * HGRN_WIDTH, HGRN_WIDTH)
    zz = _proj(h, w_ref, 3 * HGRN_WIDTH, HGRN_WIDTH)
    hw = hw_ref[...]
    for hd in range(HGRN_HEADS):
        cs = slice(hd * HGRN_DK, (hd + 1) * HGRN_DK)
        st = st_ref[hd]
        for c in range(tb // CHUNK):
            rs = slice(c * CHUNK, (c + 1) * CHUNK)
            o, st = _hgrn_chunk(qq[rs, cs], kk[rs, cs], vv[rs, cs], g[rs, cs], st)
            y_ref[rs, cs] = _head_rmsnorm(o, hw) * _silu(zz[rs, cs])
        st_ref[hd] = st


def _hgrn(x2, norm_w, w_b, lower_bounds, head_w, *, layer, bsz, seq, tb=256):
    nblk = seq // tb
    row = lambda b, s: (b * nblk + s, 0)
    block_bytes = tb * D_MODEL * 4 + tb * HGRN_WIDTH * 4
    resident_bytes = D_MODEL * WB_COLS * 2 + HGRN_HEADS * HGRN_DK * HGRN_DK * 4
    temp_bytes = 8 * tb * WB_COLS * 4
    return pl.pallas_call(
        functools.partial(_hgrn_kernel, layer=layer, tb=tb),
        grid=(bsz, nblk),
        in_specs=[pl.BlockSpec((tb, D_MODEL), row),
                  _resident((1, D_MODEL)),
                  _resident((D_MODEL, WB_COLS)),
                  _resident((DEPTH, HGRN_WIDTH)),
                  _resident((1, HGRN_DK))],
        out_specs=pl.BlockSpec((tb, HGRN_WIDTH), row),
        out_shape=jax.ShapeDtypeStruct((bsz * seq, HGRN_WIDTH), F32),
        scratch_shapes=[pltpu.VMEM((HGRN_HEADS, HGRN_DK, HGRN_DK), F32)],
        compiler_params=pltpu.CompilerParams(
            dimension_semantics=("arbitrary", "arbitrary"),
            vmem_limit_bytes=_vmem_limit(block_bytes, resident_bytes, temp_bytes)),
        name="hgrn",
    )(x2, norm_w, w_b, lower_bounds, head_w)


def _l2norm_heads(x, heads, width):
    outs = []
    for hd in range(heads):
        xh = x[:, hd * width:(hd + 1) * width]
        outs.append(xh * lax.rsqrt(jnp.sum(xh * xh, axis=-1, keepdims=True) + L2_EPS))
    return outs


def _unit_lower_inverse_rows(p2, masks):
    upper, blk16, blk32, eye_upper = masks
    n = CHUNK
    zeros = jnp.zeros((n, 2 * n), F32)
    cs = [jnp.where(blk16 & ~upper, p, 0.0) + eye_upper for p in p2]
    for _ in range(4):
        rs = [_dot(c[:, :n], c) for c in cs]
        cs = [r + jnp.where(upper, c, 0.0) for r, c in zip(rs, cs)]
    for off in (blk32 & ~blk16, ~blk32):
        ys = [_dot(jnp.where(off, p, 0.0)[:, :n], c) for p, c in zip(p2, cs)]
        rs = [_dot(c, jnp.concatenate([zeros, y], axis=0)) for c, y in zip(cs, ys)]
        cs = [c + jnp.where(upper, r, 0.0) for r, c in zip(rs, cs)]
    return cs


def _gdn_kernel(x_ref, nw_ref, w_ref, cw_ref, alog_ref, dtb_ref, hw_ref,
                y_ref, st_ref, cq_ref, ck_ref, cv_ref, *, tb):
    @pl.when(pl.program_id(1) == 0)
    def _():
        st_ref[...] = jnp.zeros_like(st_ref)
        for pad_ref in (cq_ref, ck_ref, cv_ref):
            pad_ref[0:V7X_SUBLANES, :] = jnp.zeros((V7X_SUBLANES, pad_ref.shape[1]), F32)

    h = _rmsnorm_bf16(x_ref[...], nw_ref[...])
    cw = cw_ref[...]
    qc = _silu(_causal_conv(_proj(h, w_ref, WC_Q, GDN_QK_WIDTH), cq_ref, cw[:, 0:GDN_QK_WIDTH], GDN_CONV_K))
    kc = _silu(_causal_conv(_proj(h, w_ref, WC_K, GDN_QK_WIDTH), ck_ref,
                            cw[:, GDN_QK_WIDTH:2 * GDN_QK_WIDTH], GDN_CONV_K))
    vc = _silu(_causal_conv(_proj(h, w_ref, WC_V, GDN_V_WIDTH), cv_ref, cw[:, 2 * GDN_QK_WIDTH:], GDN_CONV_K))
    qs = [x * (GDN_DK ** -0.5) for x in _l2norm_heads(qc, GDN_QK_HEADS, GDN_DK)]
    ks = _l2norm_heads(kc, GDN_QK_HEADS, GDN_DK)

    sm = _proj(h, w_ref, WC_SMALL, SMALL_COLS)
    lane = lax.broadcasted_iota(jnp.int32, sm.shape, 1)
    beta_all = _sigmoid(sm)
    log_a = -jnp.exp(alog_ref[...]) * _softplus(sm + dtb_ref[...])
    g_all = _chunk_cumsum_rows(log_a)
    comb = jnp.where(lane < GDN_V_HEADS, beta_all, g_all)
    comb_t = comb.T

    n = CHUNK
    nc = tb // CHUNK
    rep = GDN_V_HEADS // GDN_QK_HEADS
    row2 = lax.broadcasted_iota(jnp.int32, (n, 2 * n), 0)
    lane2 = lax.broadcasted_iota(jnp.int32, (n, 2 * n), 1)
    col2 = lane2 & (n - 1)
    upper = lane2 >= n
    strict2 = row2 > col2
    causal = lax.broadcasted_iota(jnp.int32, (n, n), 0) >= lax.broadcasted_iota(jnp.int32, (n, n), 1)
    blk16 = (row2 >> 4) == (col2 >> 4)
    blk32 = (row2 >> 5) == (col2 >> 5)
    eye_upper = jnp.where((row2 == col2) & upper, 1.0, 0.0).astype(F32)
    masks = (upper, blk16, blk32, eye_upper)

    p2s, aqks, rhss, qis, kst_s, decs = [], [], [], [], [], []
    for c in range(nc):
        rs = slice(c * CHUNK, (c + 1) * CHUNK)
        for hq in range(GDN_QK_HEADS):
            qh = qs[hq][rs]
            kh = ks[hq][rs]
            kk2 = _dot_nt(kh, jnp.concatenate([kh, kh], axis=0))
            qk = _dot_nt(qh, kh)
            for r in range(rep):
                vh = hq * rep + r
                vs = slice(vh * GDN_DV, (vh + 1) * GDN_DV)
                g_col = g_all[rs, GDN_V_HEADS + vh:GDN_V_HEADS + vh + 1]
                g_row = comb_t[GDN_V_HEADS + vh:GDN_V_HEADS + vh + 1, rs]
                b_col = beta_all[rs, vh:vh + 1]
                b_row = comb_t[vh:vh + 1, rs]
                g_row2 = jnp.concatenate([g_row, g_row], axis=1)
                b_row2 = jnp.concatenate([b_row, b_row], axis=1)
                dec2 = jnp.exp(jnp.minimum(g_col - g_row2, 0.0)) * b_row2
                p2s.append(jnp.where(strict2, -(kk2 * dec2), 0.0))
                aqks.append(jnp.where(causal, qk * dec2[:, :n], 0.0))
                eg = jnp.exp(g_col)
                g_last = g_col[n - 1:n, :]
                rhss.append(jnp.concatenate([vc[rs, vs], kh * eg], axis=1))
                qis.append(qh * eg)
                kst_s.append(kh * (b_col * jnp.exp(g_last - g_col)))
                decs.append(jnp.exp(g_last))

    cs = _unit_lower_inverse_rows(p2s, masks)
    zeros_rhs = jnp.zeros((n, GDN_DV + GDN_DK), F32)
    sols = [_dot(c, jnp.concatenate([zeros_rhs, rhs], axis=0)) for c, rhs in zip(cs, rhss)]

    zz = _proj(h, w_ref, WC_Z, GDN_V_WIDTH)
    hw = hw_ref[...]
    sts = [st_ref[vh] for vh in range(GDN_V_HEADS)]
    for c in range(nc):
        rs = slice(c * CHUNK, (c + 1) * CHUNK)
        idx = [c * GDN_V_HEADS + vh for vh in range(GDN_V_HEADS)]
        wq = [_dot(jnp.concatenate([sols[i][:, GDN_DV:], qis[i]], axis=0), sts[vh]) for vh, i in enumerate(idx)]
        es = [sols[i][:, :GDN_DV] - r[:n] for r, i in zip(wq, idx)]
        os_ = [r[n:] + _dot(aqks[i], e) for r, e, i in zip(wq, es, idx)]
        sts = [decs[i] * st + _dot(kst_s[i].T, e) for st, e, i in zip(sts, es, idx)]
        for vh in range(GDN_V_HEADS):
            vs = slice(vh * GDN_DV, (vh + 1) * GDN_DV)
            y_ref[rs, vs] = _head_rmsnorm(os_[vh], hw) * _silu(zz[rs, vs])
    for vh in range(GDN_V_HEADS):
        st_ref[vh] = sts[vh]


def _gdn(x2, norm_w, w_c, conv_c, alog_pad, dtb_pad, head_w, *, bsz, seq, tb=256):
    nblk = seq // tb
    row = lambda b, s: (b * nblk + s, 0)
    block_bytes = tb * D_MODEL * 4 + tb * GDN_V_WIDTH * 4
    resident_bytes = (D_MODEL * WC_COLS * 2 + GDN_CONV_K * GDN_CONV_WIDTH * 4
                      + (GDN_V_HEADS * GDN_DK * GDN_DV + (V7X_SUBLANES + tb) * GDN_CONV_WIDTH) * 4)
    temp_bytes = 6 * tb * WC_COLS * 4
    return pl.pallas_call(
        functools.partial(_gdn_kernel, tb=tb),
        grid=(bsz, nblk),
        in_specs=[
            pl.BlockSpec((tb, D_MODEL), row),
            _resident((1, D_MODEL)),
            _resident((D_MODEL, WC_COLS)),
            _resident((GDN_CONV_K, GDN_CONV_WIDTH)),
            _resident((1, SMALL_COLS)),
            _resident((1, SMALL_COLS)),
            _resident((1, GDN_DV)),
        ],
        out_specs=pl.BlockSpec((tb, GDN_V_WIDTH), row),
        out_shape=jax.ShapeDtypeStruct((bsz * seq, GDN_V_WIDTH), F32),
        scratch_shapes=[
            pltpu.VMEM((GDN_V_HEADS, GDN_DK, GDN_DV), F32),
            pltpu.VMEM((V7X_SUBLANES + tb, GDN_QK_WIDTH), F32),
            pltpu.VMEM((V7X_SUBLANES + tb, GDN_QK_WIDTH), F32),
            pltpu.VMEM((V7X_SUBLANES + tb, GDN_V_WIDTH), F32),
        ],
        compiler_params=pltpu.CompilerParams(
            dimension_semantics=("arbitrary", "arbitrary"),
            vmem_limit_bytes=_vmem_limit(block_bytes, resident_bytes, temp_bytes)),
        name="gdn",
    )(x2, norm_w, w_c, conv_c, alog_pad, dtb_pad, head_w)


def _merge_kernel(x_ref, nw_ref, w_ref, yb_ref, yc_ref, ca_ref, bg_ref, wa_ref, wb_ref, wc_ref, wo_ref, fw_ref,
                  out_ref, carry_ref, *, final):
    @pl.when(pl.program_id(1) == 0)
    def _():
        carry_ref[0:V7X_SUBLANES, :] = jnp.zeros((V7X_SUBLANES, CONV_WIDTH), F32)

    x = x_ref[...]
    h = _rmsnorm_bf16(x, nw_ref[...])
    seg = lambda i: _proj(h, w_ref, WM_A + i * CONV_WIDTH, CONV_WIDTH)
    u = seg(1) * seg(2)
    conv = _causal_conv(u, carry_ref, ca_ref[...], CONV_K)
    ya = seg(0) * conv * _silu(seg(3))

    ya2 = jnp.dot(ya.astype(BF16), wa_ref[...], preferred_element_type=F32)
    yb2 = jnp.dot(yb_ref[...].astype(BF16), wb_ref[...], preferred_element_type=F32)
    yc2 = jnp.dot(yc_ref[...].astype(BF16), wc_ref[...], preferred_element_type=F32)
    bg = bg_ref[...]
    gate = lambda i: _sigmoid(_proj(h, w_ref, WM_G + i * D_MODEL, D_MODEL) + bg[:, i * D_MODEL:(i + 1) * D_MODEL])
    merged = gate(0) * ya2 + gate(1) * yb2 + gate(2) * yc2
    out = x + jnp.dot(merged.astype(BF16), wo_ref[...], preferred_element_type=F32)
    if final:
        ms = jnp.mean(out * out, axis=-1, keepdims=True)
        out = out * lax.rsqrt(ms + NORM_EPS) * fw_ref[...]
    out_ref[...] = out


def _merge(x2, norm_w, w_m, yb, yc, conv_a, b_gate, wa, wb, wc, wo, final_w, *, final, bsz, seq, tm=256):
    nblk = seq // tm
    row = lambda b, s: (b * nblk + s, 0)
    block_bytes = tm * (2 * D_MODEL + HGRN_WIDTH + GDN_V_WIDTH) * 4
    resident_bytes = ((D_MODEL * WM_COLS + (CONV_WIDTH + HGRN_WIDTH + GDN_V_WIDTH + D_MODEL) * D_MODEL) * 2
                      + (V7X_SUBLANES + tm) * CONV_WIDTH * 4)
    temp_bytes = 3 * tm * WM_COLS * 4
    return pl.pallas_call(
        functools.partial(_merge_kernel, final=final),
        grid=(bsz, nblk),
        in_specs=[
            pl.BlockSpec((tm, D_MODEL), row),
            _resident((1, D_MODEL)),
            _resident((D_MODEL, WM_COLS)),
            pl.BlockSpec((tm, HGRN_WIDTH), row),
            pl.BlockSpec((tm, GDN_V_WIDTH), row),
            _resident((CONV_K, CONV_WIDTH)),
            _resident((1, 3 * D_MODEL)),
            _resident((CONV_WIDTH, D_MODEL)),
            _resident((HGRN_WIDTH, D_MODEL)),
            _resident((GDN_V_WIDTH, D_MODEL)),
            _resident((D_MODEL, D_MODEL)),
            _resident((1, D_MODEL)),
        ],
        out_specs=pl.BlockSpec((tm, D_MODEL), row),
        out_shape=jax.ShapeDtypeStruct((bsz * seq, D_MODEL), F32),
        scratch_shapes=[pltpu.VMEM((V7X_SUBLANES + tm, CONV_WIDTH), F32)],
        compiler_params=pltpu.CompilerParams(
            dimension_semantics=("arbitrary", "arbitrary"),
            vmem_limit_bytes=_vmem_limit(block_bytes, resident_bytes, temp_bytes)),
        name="merge_final" if final else "merge",
    )(x2, norm_w, w_m, yb, yc, conv_a, b_gate, wa, wb, wc, wo, final_w)


def kernel(x, norm_w, w_in, b_gate, conv_a, conv_c, a_log, dt_bias, lower_bounds, hgrn_norm_w, gdn_norm_w,
           w_out_a, w_out_b, w_out_c, w_o, final_norm_w):
    bsz, seq, _ = x.shape
    x2 = x.reshape(bsz * seq, D_MODEL)
    pad8 = lambda a: jnp.pad(a.astype(F32)[None, :], ((0, 0), (GDN_V_HEADS, SMALL_COLS - 2 * GDN_V_HEADS)))
    for l in range(DEPTH):
        w_l = w_in[l]
        nw = norm_w[l][None, :]
        w_b = w_l[:, SRC_B:SRC_B + WB_COLS].astype(BF16)
        w_c = jnp.concatenate(
            [w_l[:, SRC_CQKV:SRC_CQKV + GDN_CONV_WIDTH], w_l[:, SRC_CZ:SRC_CZ + GDN_V_WIDTH],
             jnp.pad(w_l[:, SRC_SMALL:SRC_SMALL + 2 * GDN_V_HEADS], ((0, 0), (0, SMALL_COLS - 2 * GDN_V_HEADS)))],
            axis=1).astype(BF16)
        w_m = jnp.concatenate([w_l[:, SRC_A:SRC_A + 4 * CONV_WIDTH], w_l[:, SRC_G:SRC_G + 3 * D_MODEL]],
                              axis=1).astype(BF16)
        yb = _hgrn(x2, nw, w_b, lower_bounds, hgrn_norm_w[l][None, :], layer=l, bsz=bsz, seq=seq)
        yc = _gdn(x2, nw, w_c, conv_c[l], pad8(a_log[l]), pad8(dt_bias[l]), gdn_norm_w[l][None, :],
                  bsz=bsz, seq=seq)
        x2 = _merge(x2, nw, w_m, yb, yc, conv_a[l], b_gate[l][None, :],
                    w_out_a[l].astype(BF16), w_out_b[l].astype(BF16), w_out_c[l].astype(BF16),
                    w_o[l].astype(BF16), final_norm_w[None, :],
                    final=(l == DEPTH - 1), bsz=bsz, seq=seq)
    return x2.reshape(bsz, seq, D_MODEL)
```

```python
import functools

import jax
import jax.numpy as jnp
from jax import lax
from jax.experimental import pallas as pl
from jax.experimental.pallas import tpu as pltpu

F32 = jnp.float32
BF16 = jnp.bfloat16

D_MODEL = 1024
DEPTH = 2
CHUNK = 64
NORM_EPS = 1e-6
L2_EPS = 1e-6
MIN_F = 1e-30
CONV_WIDTH = 512
CONV_K = 3
HGRN_HEADS = 4
HGRN_DK = 128
HGRN_WIDTH = HGRN_HEADS * HGRN_DK
GDN_QK_HEADS = 4
GDN_V_HEADS = 8
GDN_DK = 128
GDN_DV = 128
GDN_CONV_K = 4
GDN_QK_WIDTH = GDN_QK_HEADS * GDN_DK
GDN_V_WIDTH = GDN_V_HEADS * GDN_DV
GDN_CONV_WIDTH = 2 * GDN_QK_WIDTH + GDN_V_WIDTH

SRC_A = 0
SRC_B = SRC_A + 4 * CONV_WIDTH
SRC_CQKV = SRC_B + 4 * HGRN_WIDTH
SRC_SMALL = SRC_CQKV + GDN_CONV_WIDTH
SRC_CZ = SRC_SMALL + 2 * GDN_V_HEADS
SRC_G = SRC_CZ + GDN_V_WIDTH
SMALL_COLS = 128

WB_COLS = 4 * HGRN_WIDTH
WC_Q, WC_K, WC_V, WC_Z, WC_SMALL = 0, 512, 1024, 2048, 3072
WC_COLS = WC_SMALL + SMALL_COLS
WM_A, WM_G = 0, 4 * CONV_WIDTH
WM_COLS = WM_G + 3 * D_MODEL

V7X_SUBLANES = 8
V7X_VMEM_BYTES = 64 * 1024 * 1024


def _vmem_limit(block_bytes, resident_bytes, temp_bytes):
    need = 2 * block_bytes + resident_bytes + temp_bytes
    return int(min(V7X_VMEM_BYTES - 8 * 1024 * 1024, need + need // 4 + 4 * 1024 * 1024))


def _resident(shape):
    return pl.BlockSpec(shape, lambda b, s: (0,) * len(shape), pipeline_mode=pl.Buffered(1))


def _layer_rows(width):
    return _resident((DEPTH, width))


def _layer_slab(layer, shape):
    return pl.BlockSpec((None,) + tuple(shape), lambda b, s: (layer,) + (0,) * len(shape),
                        pipeline_mode=pl.Buffered(1))


def _sigmoid(x):
    return 1.0 / (1.0 + jnp.exp(-x))


def _silu(x):
    return x * _sigmoid(x)


def _softplus(x):
    return jnp.maximum(x, 0.0) + jnp.log(1.0 + jnp.exp(-jnp.abs(x)))


def _dot(a, b):
    return jnp.dot(a.astype(BF16), b.astype(BF16), preferred_element_type=F32)


def _dot_nt(a, b):
    return lax.dot_general(a.astype(BF16), b.astype(BF16), (((1,), (1,)), ((), ())),
                           preferred_element_type=F32)


def _rmsnorm_bf16(x, w):
    ms = jnp.mean(x * x, axis=-1, keepdims=True)
    return (x * lax.rsqrt(ms + NORM_EPS) * w).astype(BF16)


def _proj(h, w_ref, start, width):
    return jnp.dot(h, w_ref[:, start:start + width], preferred_element_type=F32)


def _causal_conv(x, pad_ref, w, width):
    n = x.shape[0]
    pad_ref[V7X_SUBLANES:V7X_SUBLANES + n, :] = x
    acc = x * w[width - 1:width, :]
    for j in range(1, width):
        acc = acc + pad_ref[V7X_SUBLANES - j:V7X_SUBLANES - j + n, :] * w[width - 1 - j:width - j, :]
    pad_ref[0:V7X_SUBLANES, :] = x[n - V7X_SUBLANES:n, :]
    return acc


def _chunk_cumsum_rows(x):
    pos = lax.broadcasted_iota(jnp.int32, x.shape, 0) & (CHUNK - 1)
    d = 1
    while d < CHUNK:
        xr = pltpu.roll(x, d, 0)
        x = x + jnp.where(pos >= d, xr, 0.0)
        d *= 2
    return x


def _chunk_last_rows(x):
    n = x.shape[0]
    parts = [jnp.broadcast_to(x[c * CHUNK + CHUNK - 1:(c + 1) * CHUNK, :], (CHUNK, x.shape[1]))
             for c in range(n // CHUNK)]
    return jnp.concatenate(parts, axis=0)


def _head_rmsnorm(o, w):
    return o * lax.rsqrt(jnp.mean(o * o, axis=-1, keepdims=True) + NORM_EPS) * w


HGRN_LEVELS = (32, 16, 8, 4, 2)


def _hgrn_masks():
    n = CHUNK
    rowi = lax.broadcasted_iota(jnp.int32, (n, n), 0)
    coli = lax.broadcasted_iota(jnp.int32, (n, n), 1)
    levels = []
    for half in HGRN_LEVELS:
        blk = 2 * half
        levels.append(((rowi & -blk) == (coli & -blk)) & ((rowi & (blk - 1)) >= half) & ((coli & (blk - 1)) < half))
    row2 = lax.broadcasted_iota(jnp.int32, (n, 2 * n), 0)
    lane2 = lax.broadcasted_iota(jnp.int32, (n, 2 * n), 1)
    pair = (lane2 == row2) | ((lane2 == row2 + (n - 1)) & ((row2 & 1) == 1))
    sub = lax.broadcasted_iota(jnp.int32, (n, HGRN_DK), 0) & (V7X_SUBLANES - 1)
    return levels, pair, sub


def _level_ref_rows(g, half, sub):
    n = CHUNK
    g3 = g.reshape(n // 8, 8, HGRN_DK)
    bcast8 = lambda r: jnp.broadcast_to(g3[:, r:r + 1, :], g3.shape).reshape(n, HGRN_DK)
    if half >= 8:
        blk = 2 * half
        nb = n // blk
        return jnp.broadcast_to(g.reshape(nb, blk, HGRN_DK)[:, half:half + 1, :],
                                (nb, blk, HGRN_DK)).reshape(n, HGRN_DK)
    if half == 4:
        return bcast8(4)
    return jnp.where(sub < 4, bcast8(2), bcast8(6))


def _hgrn_block(h, w_ref, lb_ref, hw_ref, st_ref, yb_ref, *, layer, tb, between=()):
    between = list(between)

    def run_between():
        if between:
            work = between.pop(0)
            if work is not None:
                work()

    rows = [lb_ref[i:i + 1, :] for i in range(DEPTH)]
    mx = functools.reduce(jnp.maximum, rows)
    es = [jnp.exp(r - mx) for r in rows]
    tot = functools.reduce(lambda a_, b_: a_ + b_, es)
    lb = jnp.zeros_like(tot)
    for i in range(1, layer + 1):
        lb = lb + es[i] / tot

    f_gate = lb + (1.0 - lb) * _sigmoid(_proj(h, w_ref, HGRN_WIDTH, HGRN_WIDTH))
    f_floor = jnp.maximum(f_gate, MIN_F)
    g_all = _chunk_cumsum_rows(jnp.log(f_floor))
    k_all = 1.0 - f_gate
    kp_all = k_all * pltpu.roll(f_floor, tb - 1, 0)
    q_all = _silu(_proj(h, w_ref, 0, HGRN_WIDTH)) * (HGRN_DK ** -0.5)
    v_all = _proj(h, w_ref, 2 * HGRN_WIDTH, HGRN_WIDTH)
    g_last_all = _chunk_last_rows(g_all)
    qi_all = q_all * jnp.exp(g_all)
    ks_all = k_all * jnp.exp(g_last_all - g_all)
    dec_all = jnp.exp(g_last_all)
    run_between()

    n = CHUNK
    nc = tb // CHUNK
    level_masks, pair_mask, sub = _hgrn_masks()
    tiles = [(slice(c * CHUNK, (c + 1) * CHUNK), slice(hd * HGRN_DK, (hd + 1) * HGRN_DK))
             for hd in range(HGRN_HEADS) for c in range(nc)]
    qs = [q_all[t] for t in tiles]
    ks = [k_all[t] for t in tiles]
    gs = [g_all[t] for t in tiles]
    vs = [v_all[t] for t in tiles]
    scores = [None] * len(tiles)
    for half, mask in zip(HGRN_LEVELS, level_masks):
        es = [jnp.exp(-jnp.abs(g - _level_ref_rows(g, half, sub))) for g in gs]
        ss = [_dot_nt(q * e, k * e) for q, k, e in zip(qs, ks, es)]
        scores = [jnp.where(mask, s, 0.0 if a is None else a) for s, a in zip(ss, scores)]
        run_between()
    s01 = [jnp.where(pair_mask, _dot_nt(q, jnp.concatenate([k, kp_all[t]], axis=0)), 0.0)
           for q, k, t in zip(qs, ks, tiles)]
    scores = [a + (s + pltpu.roll(s, n, 1))[:, :n] for a, s in zip(scores, s01)]
    intra = [_dot(a, v) for a, v in zip(scores, vs)]
    upd = [_dot(v.T, ks_all[t]) for v, t in zip(vs, tiles)]
    run_between()

    zz = _proj(h, w_ref, 3 * HGRN_WIDTH, HGRN_WIDTH)
    hw = hw_ref[layer:layer + 1, :]
    sts = [st_ref[hd] for hd in range(HGRN_HEADS)]
    for c in range(nc):
        for hd in range(HGRN_HEADS):
            i = hd * nc + c
            rs, cs = tiles[i]
            o = intra[i] + _dot_nt(qi_all[rs, cs], sts[hd])
            sts[hd] = dec_all[c * CHUNK:c * CHUNK + 1, cs] * sts[hd] + upd[i]
            yb_ref[rs, cs] = _head_rmsnorm(o, hw) * _silu(zz[rs, cs])
        run_between()
    for hd in range(HGRN_HEADS):
        st_ref[hd] = sts[hd]
    while between:
        run_between()


def _l2norm_heads(x, heads, width):
    outs = []
    for hd in range(heads):
        xh = x[:, hd * width:(hd + 1) * width]
        outs.append(xh * lax.rsqrt(jnp.sum(xh * xh, axis=-1, keepdims=True) + L2_EPS))
    return outs


def _unit_lower_inverse_rows(p2, masks):
    upper, blk16, blk32, eye_lower, eye_upper = masks
    n = CHUNK
    zero_eye = eye_upper.astype(BF16)
    cs = [jnp.where(blk16 & ~upper, p, 0.0) + eye_upper for p in p2]
    for _ in range(4):
        cbs = [c.astype(BF16) for c in cs]
        cs = [jnp.dot(cb, jnp.concatenate([cb, zero_eye], axis=0), preferred_element_type=F32) for cb in cbs]
    for off in (blk32 & ~blk16, ~blk32):
        ys = [_dot(jnp.where(off, p, 0.0)[:, :n], c) for p, c in zip(p2, cs)]
        cs = [_dot(c + eye_lower, jnp.concatenate([c, y], axis=0)) for c, y in zip(cs, ys)]
    return cs


def _gdn_kernel(x_ref, nw_ref, w_ref, cw_ref, alog_ref, dtb_ref, hw_ref,
                y_ref, st_ref, cq_ref, ck_ref, cv_ref, *, layer, tb):
    @pl.when(pl.program_id(1) == 0)
    def _():
        st_ref[...] = jnp.zeros_like(st_ref)
        for pad_ref in (cq_ref, ck_ref, cv_ref):
            pad_ref[0:V7X_SUBLANES, :] = jnp.zeros((V7X_SUBLANES, pad_ref.shape[1]), F32)

    h = _rmsnorm_bf16(x_ref[...], nw_ref[layer:layer + 1, :])
    cw = cw_ref[...]
    qc = _silu(_causal_conv(_proj(h, w_ref, WC_Q, GDN_QK_WIDTH), cq_ref, cw[:, 0:GDN_QK_WIDTH], GDN_CONV_K))
    kc = _silu(_causal_conv(_proj(h, w_ref, WC_K, GDN_QK_WIDTH), ck_ref,
                            cw[:, GDN_QK_WIDTH:2 * GDN_QK_WIDTH], GDN_CONV_K))
    vc = _silu(_causal_conv(_proj(h, w_ref, WC_V, GDN_V_WIDTH), cv_ref, cw[:, 2 * GDN_QK_WIDTH:], GDN_CONV_K))
    qs = [x * (GDN_DK ** -0.5) for x in _l2norm_heads(qc, GDN_QK_HEADS, GDN_DK)]
    ks = _l2norm_heads(kc, GDN_QK_HEADS, GDN_DK)

    sm = _proj(h, w_ref, WC_SMALL, SMALL_COLS)
    lane = lax.broadcasted_iota(jnp.int32, sm.shape, 1)
    beta_all = _sigmoid(sm)
    log_a = -jnp.exp(alog_ref[layer:layer + 1, :]) * _softplus(sm + dtb_ref[layer:layer + 1, :])
    g_all = _chunk_cumsum_rows(log_a)
    comb_t = jnp.where(lane < GDN_V_HEADS, beta_all, g_all).T
    g_last_all = _chunk_last_rows(g_all)
    eg_all = jnp.exp(g_all)
    kfac_all = pltpu.roll(beta_all, GDN_V_HEADS, 1) * jnp.exp(g_last_all - g_all)
    dec_all = jnp.exp(g_last_all)

    n = CHUNK
    nc = tb // CHUNK
    rep = GDN_V_HEADS // GDN_QK_HEADS
    row2 = lax.broadcasted_iota(jnp.int32, (n, 2 * n), 0)
    lane2 = lax.broadcasted_iota(jnp.int32, (n, 2 * n), 1)
    col2 = lane2 & (n - 1)
    upper = lane2 >= n
    strict2 = row2 > col2
    causal = lax.broadcasted_iota(jnp.int32, (n, n), 0) >= lax.broadcasted_iota(jnp.int32, (n, n), 1)
    blk16 = (row2 >> 4) == (col2 >> 4)
    blk32 = (row2 >> 5) == (col2 >> 5)
    eye_lower = jnp.where((row2 == col2) & ~upper, 1.0, 0.0).astype(F32)
    eye_upper = jnp.where((row2 == col2) & upper, 1.0, 0.0).astype(F32)
    masks = (upper, blk16, blk32, eye_lower, eye_upper)

    p2s, aqks, rhss, qis, kst_s, decs = [], [], [], [], [], []
    for c in range(nc):
        rs = slice(c * CHUNK, (c + 1) * CHUNK)
        for hq in range(GDN_QK_HEADS):
            qh = qs[hq][rs]
            kh = ks[hq][rs]
            nkk2 = _dot_nt(-kh, jnp.concatenate([kh, kh], axis=0))
            qk = _dot_nt(qh, kh)
            for r in range(rep):
                vh = hq * rep + r
                vs = slice(vh * GDN_DV, (vh + 1) * GDN_DV)
                gl = GDN_V_HEADS + vh
                g_col = g_all[rs, gl:gl + 1]
                g_row = comb_t[gl:gl + 1, rs]
                b_row = comb_t[vh:vh + 1, rs]
                g_row2 = jnp.concatenate([g_row, g_row], axis=1)
                b_row2 = jnp.concatenate([b_row, b_row], axis=1)
                dec2 = jnp.exp(jnp.minimum(g_col - g_row2, 0.0)) * b_row2
                p2s.append(jnp.where(strict2, nkk2 * dec2, 0.0))
                aqks.append(jnp.where(causal, qk * dec2[:, :n], 0.0))
                eg = eg_all[rs, gl:gl + 1]
                rhss.append(jnp.concatenate([vc[rs, vs], kh * eg], axis=1))
                qis.append(qh * eg)
                kst_s.append(kh * kfac_all[rs, gl:gl + 1])
                decs.append(dec_all[c * CHUNK:c * CHUNK + 1, gl:gl + 1])

    cs = _unit_lower_inverse_rows(p2s, masks)
    zeros_rhs = jnp.zeros((n, GDN_DV + GDN_DK), F32)
    sols = [_dot(c, jnp.concatenate([zeros_rhs, rhs], axis=0)) for c, rhs in zip(cs, rhss)]

    zz = _proj(h, w_ref, WC_Z, GDN_V_WIDTH)
    hw = hw_ref[layer:layer + 1, :]
    sts = [st_ref[vh] for vh in range(GDN_V_HEADS)]
    for c in range(nc):
        rs = slice(c * CHUNK, (c + 1) * CHUNK)
        idx = [c * GDN_V_HEADS + vh for vh in range(GDN_V_HEADS)]
        wq = [_dot(jnp.concatenate([sols[i][:, GDN_DV:], qis[i]], axis=0), sts[vh]) for vh, i in enumerate(idx)]
        es = [sols[i][:, :GDN_DV] - r[:n] for r, i in zip(wq, idx)]
        os_ = [r[n:] + _dot(aqks[i], e) for r, e, i in zip(wq, es, idx)]
        sts = [decs[i] * st + _dot(kst_s[i].T, e) for st, e, i in zip(sts, es, idx)]
        for vh in range(GDN_V_HEADS):
            vs = slice(vh * GDN_DV, (vh + 1) * GDN_DV)
            y_ref[rs, vs] = _head_rmsnorm(os_[vh], hw) * _silu(zz[rs, vs])
    for vh in range(GDN_V_HEADS):
        st_ref[vh] = sts[vh]


def _gdn(x2, norm_w, w_c, conv_c, alog_pad, dtb_pad, head_w, *, layer, bsz, seq, tb=256):
    nblk = seq // tb
    row = lambda b, s: (b * nblk + s, 0)
    block_bytes = tb * D_MODEL * 4 + tb * GDN_V_WIDTH * 4
    resident_bytes = (D_MODEL * WC_COLS * 2 + GDN_CONV_K * GDN_CONV_WIDTH * 4
                      + (GDN_V_HEADS * GDN_DK * GDN_DV + (V7X_SUBLANES + tb) * GDN_CONV_WIDTH) * 4)
    temp_bytes = 6 * tb * WC_COLS * 4
    return pl.pallas_call(
        functools.partial(_gdn_kernel, layer=layer, tb=tb),
        grid=(bsz, nblk),
        in_specs=[
            pl.BlockSpec((tb, D_MODEL), row),
            _layer_rows(D_MODEL),
            _layer_slab(layer, (D_MODEL, WC_COLS)),
            _layer_slab(layer, (GDN_CONV_K, GDN_CONV_WIDTH)),
            _layer_rows(SMALL_COLS),
            _layer_rows(SMALL_COLS),
            _layer_rows(GDN_DV),
        ],
        out_specs=pl.BlockSpec((tb, GDN_V_WIDTH), row),
        out_shape=jax.ShapeDtypeStruct((bsz * seq, GDN_V_WIDTH), F32),
        scratch_shapes=[
            pltpu.VMEM((GDN_V_HEADS, GDN_DK, GDN_DV), F32),
            pltpu.VMEM((V7X_SUBLANES + tb, GDN_QK_WIDTH), F32),
            pltpu.VMEM((V7X_SUBLANES + tb, GDN_QK_WIDTH), F32),
            pltpu.VMEM((V7X_SUBLANES + tb, GDN_V_WIDTH), F32),
        ],
        compiler_params=pltpu.CompilerParams(
            dimension_semantics=("arbitrary", "arbitrary"),
            vmem_limit_bytes=_vmem_limit(block_bytes, resident_bytes, temp_bytes)),
        name="gdn",
    )(x2, norm_w, w_c, conv_c, alog_pad, dtb_pad, head_w)


def _hgrn_merge_kernel(x_ref, nw_ref, wb_ref, lb_ref, hw_ref, wm_ref, yc_ref, ca_ref, bg_ref,
                       wa_ref, wbo_ref, wc_ref, wo_ref, fw_ref,
                       out_ref, st_ref, pad_ref, yb_ref, *, layer, final, tb):
    @pl.when(pl.program_id(1) == 0)
    def _():
        st_ref[...] = jnp.zeros_like(st_ref)
        pad_ref[0:V7X_SUBLANES, :] = jnp.zeros((V7X_SUBLANES, CONV_WIDTH), F32)

    x = x_ref[...]
    h = _rmsnorm_bf16(x, nw_ref[layer:layer + 1, :])
    seg = lambda i: _proj(h, wm_ref, WM_A + i * CONV_WIDTH, CONV_WIDTH)
    bg = bg_ref[layer:layer + 1, :]
    gate = lambda i: _sigmoid(_proj(h, wm_ref, WM_G + i * D_MODEL, D_MODEL) + bg[:, i * D_MODEL:(i + 1) * D_MODEL])
    part = {}

    def conv_a():
        part["conv"] = _causal_conv(seg(1) * seg(2), pad_ref, ca_ref[...], CONV_K)

    def branch_a():
        ya = seg(0) * part["conv"] * _silu(seg(3))
        part["ya2"] = jnp.dot(ya.astype(BF16), wa_ref[...], preferred_element_type=F32)

    def branch_c():
        part["yc2"] = jnp.dot(yc_ref[...].astype(BF16), wc_ref[...], preferred_element_type=F32)

    def gate_a():
        part["merged"] = gate(0) * part["ya2"]

    def gate_c():
        part["merged"] = part["merged"] + gate(2) * part["yc2"]

    def gate_b():
        part["gate_b"] = gate(1)

    _hgrn_block(h, wb_ref, lb_ref, hw_ref, st_ref, yb_ref, layer=layer, tb=tb,
                between=(None, conv_a, branch_a, branch_c, None, gate_a, None, gate_c, gate_b))

    yb2 = jnp.dot(yb_ref[...].astype(BF16), wbo_ref[...], preferred_element_type=F32)
    merged = part["merged"] + part["gate_b"] * yb2
    out = x + jnp.dot(merged.astype(BF16), wo_ref[...], preferred_element_type=F32)
    if final:
        ms = jnp.mean(out * out, axis=-1, keepdims=True)
        out = out * lax.rsqrt(ms + NORM_EPS) * fw_ref[...]
    out_ref[...] = out


def _hgrn_merge(x2, norm_w, w_b, lower_bounds, head_w, w_m, yc, conv_a, b_gate, wa, wb, wc, wo, final_w,
                *, layer, final, bsz, seq, tb=256):
    nblk = seq // tb
    row = lambda b, s: (b * nblk + s, 0)
    block_bytes = tb * (2 * D_MODEL + GDN_V_WIDTH) * 4
    resident_bytes = ((D_MODEL * (WB_COLS + WM_COLS) + (CONV_WIDTH + HGRN_WIDTH + GDN_V_WIDTH + D_MODEL) * D_MODEL) * 2
                      + (HGRN_HEADS * HGRN_DK * HGRN_DK + (V7X_SUBLANES + tb) * CONV_WIDTH + tb * HGRN_WIDTH) * 4)
    temp_bytes = 4 * tb * (WB_COLS + WM_COLS) * 4
    return pl.pallas_call(
        functools.partial(_hgrn_merge_kernel, layer=layer, final=final, tb=tb),
        grid=(bsz, nblk),
        in_specs=[
            pl.BlockSpec((tb, D_MODEL), row),
            _layer_rows(D_MODEL),
            _layer_slab(layer, (D_MODEL, WB_COLS)),
            _resident((DEPTH, HGRN_WIDTH)),
            _layer_rows(HGRN_DK),
            _layer_slab(layer, (D_MODEL, WM_COLS)),
            pl.BlockSpec((tb, GDN_V_WIDTH), row),
            _layer_slab(layer, (CONV_K, CONV_WIDTH)),
            _layer_rows(3 * D_MODEL),
            _layer_slab(layer, (CONV_WIDTH, D_MODEL)),
            _layer_slab(layer, (HGRN_WIDTH, D_MODEL)),
            _layer_slab(layer, (GDN_V_WIDTH, D_MODEL)),
            _layer_slab(layer, (D_MODEL, D_MODEL)),
            _resident((1, D_MODEL)),
        ],
        out_specs=pl.BlockSpec((tb, D_MODEL), row),
        out_shape=jax.ShapeDtypeStruct((bsz * seq, D_MODEL), F32),
        scratch_shapes=[
            pltpu.VMEM((HGRN_HEADS, HGRN_DK, HGRN_DK), F32),
            pltpu.VMEM((V7X_SUBLANES + tb, CONV_WIDTH), F32),
            pltpu.VMEM((tb, HGRN_WIDTH), F32),
        ],
        compiler_params=pltpu.CompilerParams(
            dimension_semantics=("arbitrary", "arbitrary"),
            vmem_limit_bytes=_vmem_limit(block_bytes, resident_bytes, temp_bytes)),
        name="hgrn_merge_final" if final else "hgrn_merge",
    )(x2, norm_w, w_b, lower_bounds, head_w, w_m, yc, conv_a, b_gate, wa, wb, wc, wo, final_w)


PREP_ROWS = 128


def _prep_kernel(w_ref, wa_ref, wb_ref, wc_ref, wo_ref, pb_ref, pc_ref, pm_ref, oa_ref, ob_ref, oc_ref, oo_ref):
    w = w_ref[...]
    pb_ref[...] = w[:, SRC_B:SRC_B + WB_COLS].astype(BF16)
    pc_ref[:, WC_Q:WC_Z] = w[:, SRC_CQKV:SRC_CQKV + GDN_CONV_WIDTH].astype(BF16)
    pc_ref[:, WC_Z:WC_SMALL] = w[:, SRC_CZ:SRC_CZ + GDN_V_WIDTH].astype(BF16)
    lane = lax.broadcasted_iota(jnp.int32, (PREP_ROWS, SMALL_COLS), 1)
    pc_ref[:, WC_SMALL:] = jnp.where(lane < 2 * GDN_V_HEADS, w[:, SRC_SMALL:SRC_SMALL + SMALL_COLS], 0.0).astype(BF16)
    pm_ref[:, WM_A:WM_G] = w[:, SRC_A:SRC_A + 4 * CONV_WIDTH].astype(BF16)
    pm_ref[:, WM_G:] = w[:, SRC_G:SRC_G + 3 * D_MODEL].astype(BF16)
    oa_ref[...] = wa_ref[...].astype(BF16)
    ob_ref[...] = wb_ref[...].astype(BF16)
    oc_ref[...] = wc_ref[...].astype(BF16)
    oo_ref[...] = wo_ref[...].astype(BF16)


def _prep_weights(w_in, w_out_a, w_out_b, w_out_c, w_o):
    in_cols = w_in.shape[-1]
    steps = D_MODEL // PREP_ROWS
    slab = lambda rows, cols: pl.BlockSpec((None, rows // steps, cols), lambda l, i: (l, i, 0))
    out = lambda rows, cols: jax.ShapeDtypeStruct((DEPTH, rows, cols), BF16)
    block_bytes = (PREP_ROWS * (in_cols * 4 + (WB_COLS + WC_COLS + WM_COLS) * 2)
                   + (CONV_WIDTH + HGRN_WIDTH + GDN_V_WIDTH + D_MODEL) // steps * D_MODEL * 6)
    return pl.pallas_call(
        _prep_kernel,
        grid=(DEPTH, steps),
        in_specs=[slab(D_MODEL, in_cols), slab(CONV_WIDTH, D_MODEL), slab(HGRN_WIDTH, D_MODEL),
                  slab(GDN_V_WIDTH, D_MODEL), slab(D_MODEL, D_MODEL)],
        out_specs=[slab(D_MODEL, WB_COLS), slab(D_MODEL, WC_COLS), slab(D_MODEL, WM_COLS),
                   slab(CONV_WIDTH, D_MODEL), slab(HGRN_WIDTH, D_MODEL), slab(GDN_V_WIDTH, D_MODEL),
                   slab(D_MODEL, D_MODEL)],
        out_shape=[out(D_MODEL, WB_COLS), out(D_MODEL, WC_COLS), out(D_MODEL, WM_COLS),
                   out(CONV_WIDTH, D_MODEL), out(HGRN_WIDTH, D_MODEL), out(GDN_V_WIDTH, D_MODEL),
                   out(D_MODEL, D_MODEL)],
        compiler_params=pltpu.CompilerParams(
            dimension_semantics=("arbitrary", "arbitrary"),
            vmem_limit_bytes=_vmem_limit(block_bytes, 0, 2 * PREP_ROWS * in_cols * 4)),
        name="prep_weights",
    )(w_in, w_out_a, w_out_b, w_out_c, w_o)


def kernel(x, norm_w, w_in, b_gate, conv_a, conv_c, a_log, dt_bias, lower_bounds, hgrn_norm_w, gdn_norm_w,
           w_out_a, w_out_b, w_out_c, w_o, final_norm_w):
    bsz, seq, _ = x.shape
    x2 = x.reshape(bsz * seq, D_MODEL)
    w_b, w_c, w_m, wa, wb, wc, wo = _prep_weights(w_in, w_out_a, w_out_b, w_out_c, w_o)
    pad8 = lambda a: jnp.pad(a.astype(F32), ((0, 0), (GDN_V_HEADS, SMALL_COLS - 2 * GDN_V_HEADS)))
    alog_pad, dtb_pad = pad8(a_log), pad8(dt_bias)
    final_w = final_norm_w[None, :]
    for l in range(DEPTH):
        yc = _gdn(x2, norm_w, w_c, conv_c, alog_pad, dtb_pad, gdn_norm_w, layer=l, bsz=bsz, seq=seq)
        x2 = _hgrn_merge(x2, norm_w, w_b, lower_bounds, hgrn_norm_w, w_m, yc, conv_a, b_gate, wa, wb, wc, wo,
                         final_w, layer=l, final=(l == DEPTH - 1), bsz=bsz, seq=seq)
    return x2.reshape(bsz, seq, D_MODEL)
```

```python
import functools

import jax
import jax.numpy as jnp
from jax import lax
from jax.experimental import pallas as pl
from jax.experimental.pallas import tpu as pltpu

F32 = jnp.float32
BF16 = jnp.bfloat16

D_MODEL = 1024
DEPTH = 2
CHUNK = 64
NORM_EPS = 1e-6
L2_EPS = 1e-6
MIN_F = 1e-30
CONV_WIDTH = 512
CONV_K = 3
HGRN_HEADS = 4
HGRN_DK = 128
HGRN_WIDTH = HGRN_HEADS * HGRN_DK
GDN_QK_HEADS = 4
GDN_V_HEADS = 8
GDN_DK = 128
GDN_DV = 128
GDN_CONV_K = 4
GDN_QK_WIDTH = GDN_QK_HEADS * GDN_DK
GDN_V_WIDTH = GDN_V_HEADS * GDN_DV
GDN_CONV_WIDTH = 2 * GDN_QK_WIDTH + GDN_V_WIDTH

SRC_A = 0
SRC_B = SRC_A + 4 * CONV_WIDTH
SRC_CQKV = SRC_B + 4 * HGRN_WIDTH
SRC_SMALL = SRC_CQKV + GDN_CONV_WIDTH
SRC_CZ = SRC_SMALL + 2 * GDN_V_HEADS
SRC_G = SRC_CZ + GDN_V_WIDTH
SMALL_COLS = 128

WB_COLS = 4 * HGRN_WIDTH
WC_Q, WC_K, WC_V, WC_Z, WC_SMALL = 0, 512, 1024, 2048, 3072
WC_COLS = WC_SMALL + SMALL_COLS
WM_A, WM_G = 0, 4 * CONV_WIDTH
WM_COLS = WM_G + 3 * D_MODEL

V7X_SUBLANES = 8
V7X_VMEM_BYTES = 64 * 1024 * 1024


def _vmem_limit(block_bytes, resident_bytes, temp_bytes):
    need = 2 * block_bytes + resident_bytes + temp_bytes
    return int(min(V7X_VMEM_BYTES - 8 * 1024 * 1024, need + need // 4 + 4 * 1024 * 1024))


def _resident(shape):
    return pl.BlockSpec(shape, lambda b, s: (0,) * len(shape), pipeline_mode=pl.Buffered(1))


def _layer_rows(width):
    return _resident((DEPTH, width))


def _layer_slab(layer, shape):
    return pl.BlockSpec((None,) + tuple(shape), lambda b, s: (layer,) + (0,) * len(shape),
                        pipeline_mode=pl.Buffered(1))


def _sigmoid(x):
    return 0.5 * jnp.tanh(0.5 * x) + 0.5


def _silu(x):
    return x * _sigmoid(x)


def _softplus(x):
    return jnp.maximum(x, 0.0) + jnp.log(1.0 + jnp.exp(-jnp.abs(x)))


def _dot(a, b):
    return jnp.dot(a.astype(BF16), b.astype(BF16), preferred_element_type=F32)


def _dot_nt(a, b):
    return lax.dot_general(a.astype(BF16), b.astype(BF16), (((1,), (1,)), ((), ())),
                           preferred_element_type=F32)


def _rmsnorm_bf16(x, w):
    ms = jnp.mean(x * x, axis=-1, keepdims=True)
    return (x * lax.rsqrt(ms + NORM_EPS) * w).astype(BF16)


def _proj(h, w_ref, start, width):
    return jnp.dot(h, w_ref[:, start:start + width], preferred_element_type=F32)


def _causal_conv(x, pad_ref, w, width):
    n = x.shape[0]
    pad_ref[V7X_SUBLANES:V7X_SUBLANES + n, :] = x
    acc = x * w[width - 1:width, :]
    for j in range(1, width):
        acc = acc + pad_ref[V7X_SUBLANES - j:V7X_SUBLANES - j + n, :] * w[width - 1 - j:width - j, :]
    pad_ref[0:V7X_SUBLANES, :] = x[n - V7X_SUBLANES:n, :]
    return acc


def _chunk_cumsum_rows(x):
    pos = lax.broadcasted_iota(jnp.int32, x.shape, 0) & (CHUNK - 1)
    d = 1
    while d < CHUNK:
        xr = pltpu.roll(x, d, 0)
        x = x + jnp.where(pos >= d, xr, 0.0)
        d *= 2
    return x


def _chunk_last_rows(x):
    n = x.shape[0]
    parts = [jnp.broadcast_to(x[c * CHUNK + CHUNK - 1:(c + 1) * CHUNK, :], (CHUNK, x.shape[1]))
             for c in range(n // CHUNK)]
    return jnp.concatenate(parts, axis=0)


def _head_rmsnorm(o, w):
    return o * lax.rsqrt(jnp.mean(o * o, axis=-1, keepdims=True) + NORM_EPS) * w


HGRN_LEVELS = (32, 16, 8, 4, 2)


def _hgrn_masks():
    n = CHUNK
    rowi = lax.broadcasted_iota(jnp.int32, (n, n), 0)
    coli = lax.broadcasted_iota(jnp.int32, (n, n), 1)
    levels = []
    for half in HGRN_LEVELS:
        blk = 2 * half
        levels.append(((rowi & -blk) == (coli & -blk)) & ((rowi & (blk - 1)) >= half) & ((coli & (blk - 1)) < half))
    row2 = lax.broadcasted_iota(jnp.int32, (n, 2 * n), 0)
    lane2 = lax.broadcasted_iota(jnp.int32, (n, 2 * n), 1)
    pair = (lane2 == row2) | ((lane2 == row2 + (n - 1)) & ((row2 & 1) == 1))
    sub = lax.broadcasted_iota(jnp.int32, (n, HGRN_DK), 0) & (V7X_SUBLANES - 1)
    return levels, pair, sub


def _level_ref_rows(g, half, sub):
    n = CHUNK
    g3 = g.reshape(n // 8, 8, HGRN_DK)
    bcast8 = lambda r: jnp.broadcast_to(g3[:, r:r + 1, :], g3.shape).reshape(n, HGRN_DK)
    if half >= 8:
        blk = 2 * half
        nb = n // blk
        return jnp.broadcast_to(g.reshape(nb, blk, HGRN_DK)[:, half:half + 1, :],
                                (nb, blk, HGRN_DK)).reshape(n, HGRN_DK)
    if half == 4:
        return bcast8(4)
    return jnp.where(sub < 4, bcast8(2), bcast8(6))


def _hgrn_block(h, w_ref, lb_ref, hw_ref, st_ref, yb_ref, *, layer, tb, between=()):
    between = list(between)

    def run_between():
        if between:
            work = between.pop(0)
            if work is not None:
                work()

    rows = [lb_ref[i:i + 1, :] for i in range(DEPTH)]
    mx = functools.reduce(jnp.maximum, rows)
    es = [jnp.exp(r - mx) for r in rows]
    tot = functools.reduce(lambda a_, b_: a_ + b_, es)
    lb = jnp.zeros_like(tot)
    for i in range(1, layer + 1):
        lb = lb + es[i] / tot

    f_gate = lb + (1.0 - lb) * _sigmoid(_proj(h, w_ref, HGRN_WIDTH, HGRN_WIDTH))
    f_floor = jnp.maximum(f_gate, MIN_F)
    g_all = _chunk_cumsum_rows(jnp.log(f_floor))
    k_all = 1.0 - f_gate
    kp_all = k_all * pltpu.roll(f_floor, tb - 1, 0)
    q_all = _silu(_proj(h, w_ref, 0, HGRN_WIDTH)) * (HGRN_DK ** -0.5)
    v_all = _proj(h, w_ref, 2 * HGRN_WIDTH, HGRN_WIDTH)
    g_last_all = _chunk_last_rows(g_all)
    qi_all = q_all * jnp.exp(g_all)
    ks_all = k_all * jnp.exp(g_last_all - g_all)
    dec_all = jnp.exp(g_last_all)
    run_between()

    n = CHUNK
    nc = tb // CHUNK
    level_masks, pair_mask, sub = _hgrn_masks()
    tiles = [(slice(c * CHUNK, (c + 1) * CHUNK), slice(hd * HGRN_DK, (hd + 1) * HGRN_DK))
             for hd in range(HGRN_HEADS) for c in range(nc)]
    qs = [q_all[t] for t in tiles]
    ks = [k_all[t] for t in tiles]
    gs = [g_all[t] for t in tiles]
    vs = [v_all[t] for t in tiles]
    scores = [None] * len(tiles)
    for half, mask in zip(HGRN_LEVELS, level_masks):
        es = [jnp.exp(-jnp.abs(g - _level_ref_rows(g, half, sub))) for g in gs]
        ss = [_dot_nt(q * e, k * e) for q, k, e in zip(qs, ks, es)]
        scores = [jnp.where(mask, s, 0.0 if a is None else a) for s, a in zip(ss, scores)]
        run_between()
    s01 = [jnp.where(pair_mask, _dot_nt(q, jnp.concatenate([k, kp_all[t]], axis=0)), 0.0)
           for q, k, t in zip(qs, ks, tiles)]
    scores = [a + (s + pltpu.roll(s, n, 1))[:, :n] for a, s in zip(scores, s01)]
    intra = [_dot(a, v) for a, v in zip(scores, vs)]
    upd = [_dot(v.T, ks_all[t]) for v, t in zip(vs, tiles)]
    run_between()

    zz = _proj(h, w_ref, 3 * HGRN_WIDTH, HGRN_WIDTH)
    hw = hw_ref[layer:layer + 1, :]
    sts = [st_ref[hd] for hd in range(HGRN_HEADS)]
    for c in range(nc):
        for hd in range(HGRN_HEADS):
            i = hd * nc + c
            rs, cs = tiles[i]
            o = intra[i] + _dot_nt(qi_all[rs, cs], sts[hd])
            sts[hd] = dec_all[c * CHUNK:c * CHUNK + 1, cs] * sts[hd] + upd[i]
            yb_ref[rs, cs] = _head_rmsnorm(o, hw) * _silu(zz[rs, cs])
        run_between()
    for hd in range(HGRN_HEADS):
        st_ref[hd] = sts[hd]
    while between:
        run_between()


def _l2norm_heads(x, heads, width):
    outs = []
    for hd in range(heads):
        xh = x[:, hd * width:(hd + 1) * width]
        outs.append(xh * lax.rsqrt(jnp.sum(xh * xh, axis=-1, keepdims=True) + L2_EPS))
    return outs


def _unit_lower_inverse_rows(p2, masks):
    upper, blk16, blk32, eye_lower, eye_upper = masks
    n = CHUNK
    zero_eye = eye_upper.astype(BF16)
    cs = [jnp.where(blk16 & ~upper, p, 0.0) + eye_upper for p in p2]
    for _ in range(4):
        cbs = [c.astype(BF16) for c in cs]
        cs = [jnp.dot(cb, jnp.concatenate([cb, zero_eye], axis=0), preferred_element_type=F32) for cb in cbs]
        yield
    for off in (blk32 & ~blk16, ~blk32):
        ys = [_dot(jnp.where(off, p, 0.0)[:, :n], c) for p, c in zip(p2, cs)]
        yield
        cs = [_dot(c + eye_lower, jnp.concatenate([c, y], axis=0)) for c, y in zip(cs, ys)]
        yield
    return cs


def _interleave(*gens):
    live = list(gens)
    while live:
        for g in list(live):
            try:
                next(g)
            except StopIteration:
                live.remove(g)


def _conv_rows(pad_ref, w, rows, width):
    base = V7X_SUBLANES + rows.start
    n = rows.stop - rows.start
    acc = pad_ref[base:base + n, :] * w[width - 1:width, :]
    for j in range(1, width):
        acc = acc + pad_ref[base - j:base - j + n, :] * w[width - 1 - j:width - j, :]
    return acc


def _gdn_kernel(x_ref, nw_ref, w_ref, cw_ref, alog_ref, dtb_ref, hw_ref,
                y_ref, st_ref, cq_ref, ck_ref, cv_ref, *, layer, tb):
    @pl.when(pl.program_id(1) == 0)
    def _():
        st_ref[...] = jnp.zeros_like(st_ref)
        for pad_ref in (cq_ref, ck_ref, cv_ref):
            pad_ref[0:V7X_SUBLANES, :] = jnp.zeros((V7X_SUBLANES, pad_ref.shape[1]), F32)

    n = CHUNK
    nc = tb // CHUNK
    rep = GDN_V_HEADS // GDN_QK_HEADS
    h = _rmsnorm_bf16(x_ref[...], nw_ref[layer:layer + 1, :])
    cw = cw_ref[...]
    cq_ref[V7X_SUBLANES:, :] = _proj(h, w_ref, WC_Q, GDN_QK_WIDTH)
    ck_ref[V7X_SUBLANES:, :] = _proj(h, w_ref, WC_K, GDN_QK_WIDTH)
    cv_ref[V7X_SUBLANES:, :] = _proj(h, w_ref, WC_V, GDN_V_WIDTH)
    sm = _proj(h, w_ref, WC_SMALL, SMALL_COLS)
    zz = _proj(h, w_ref, WC_Z, GDN_V_WIDTH)
    alog = alog_ref[layer:layer + 1, :]
    dtb = dtb_ref[layer:layer + 1, :]
    hw = hw_ref[layer:layer + 1, :]

    row2 = lax.broadcasted_iota(jnp.int32, (n, 2 * n), 0)
    lane2 = lax.broadcasted_iota(jnp.int32, (n, 2 * n), 1)
    col2 = lane2 & (n - 1)
    upper = lane2 >= n
    strict2 = row2 > col2
    causal = lax.broadcasted_iota(jnp.int32, (n, n), 0) >= lax.broadcasted_iota(jnp.int32, (n, n), 1)
    blk16 = (row2 >> 4) == (col2 >> 4)
    blk32 = (row2 >> 5) == (col2 >> 5)
    eye_lower = jnp.where((row2 == col2) & ~upper, 1.0, 0.0).astype(F32)
    eye_upper = jnp.where((row2 == col2) & upper, 1.0, 0.0).astype(F32)
    masks = (upper, blk16, blk32, eye_lower, eye_upper)
    lane = lax.broadcasted_iota(jnp.int32, (n, SMALL_COLS), 1)
    zeros_rhs = jnp.zeros((n, GDN_DV + GDN_DK), F32)

    def local_stage(c):
        rs = slice(c * CHUNK, (c + 1) * CHUNK)
        qc = _silu(_conv_rows(cq_ref, cw[:, 0:GDN_QK_WIDTH], rs, GDN_CONV_K))
        kc = _silu(_conv_rows(ck_ref, cw[:, GDN_QK_WIDTH:2 * GDN_QK_WIDTH], rs, GDN_CONV_K))
        vc = _silu(_conv_rows(cv_ref, cw[:, 2 * GDN_QK_WIDTH:], rs, GDN_CONV_K))
        qs = [x * (GDN_DK ** -0.5) for x in _l2norm_heads(qc, GDN_QK_HEADS, GDN_DK)]
        ks = _l2norm_heads(kc, GDN_QK_HEADS, GDN_DK)
        smc = sm[rs]
        beta = _sigmoid(smc)
        g = _chunk_cumsum_rows(-jnp.exp(alog) * _softplus(smc + dtb))
        comb_t = jnp.where(lane < GDN_V_HEADS, beta, g).T
        g_last = g[n - 1:n, :]
        eg_all = jnp.exp(g)
        kfac_all = pltpu.roll(beta, GDN_V_HEADS, 1) * jnp.exp(g_last - g)
        dec_all = jnp.exp(g_last)
        out = dict(p2=[], aqk=[], rhs=[], qi=[], kst=[], dec=[])
        yield
        for hq in range(GDN_QK_HEADS):
            qh, kh = qs[hq], ks[hq]
            nkk2 = _dot_nt(-kh, jnp.concatenate([kh, kh], axis=0))
            qk = _dot_nt(qh, kh)
            for r in range(rep):
                vh = hq * rep + r
                gl = GDN_V_HEADS + vh
                g_row = comb_t[gl:gl + 1, :]
                b_row = comb_t[vh:vh + 1, :]
                g_row2 = jnp.concatenate([g_row, g_row], axis=1)
                b_row2 = jnp.concatenate([b_row, b_row], axis=1)
                dec2 = jnp.exp(jnp.minimum(g[:, gl:gl + 1] - g_row2, 0.0)) * b_row2
                eg = eg_all[:, gl:gl + 1]
                out["p2"].append(jnp.where(strict2, nkk2 * dec2, 0.0))
                out["aqk"].append(jnp.where(causal, qk * dec2[:, :n], 0.0))
                out["rhs"].append(jnp.concatenate([vc[:, vh * GDN_DV:(vh + 1) * GDN_DV], kh * eg], axis=1))
                out["qi"].append(qh * eg)
                out["kst"].append(kh * kfac_all[:, gl:gl + 1])
                out["dec"].append(dec_all[:, gl:gl + 1])
            yield
        locs[c] = out

    def solve_stage(loc):
        cs = yield from _unit_lower_inverse_rows(loc["p2"], masks)
        loc["sol"] = [_dot(c_, jnp.concatenate([zeros_rhs, rhs], axis=0)) for c_, rhs in zip(cs, loc["rhs"])]

    def state_stage(c, loc, sts):
        rs = slice(c * CHUNK, (c + 1) * CHUNK)
        wq = [_dot(jnp.concatenate([sol[:, GDN_DV:], qi], axis=0), st) for sol, qi, st in zip(loc["sol"], loc["qi"], sts)]
        es = [sol[:, :GDN_DV] - r[:n] for sol, r in zip(loc["sol"], wq)]
        os_ = [r[n:] + _dot(aqk, e) for r, aqk, e in zip(wq, loc["aqk"], es)]
        new = [dec * st + _dot(kst.T, e) for dec, st, kst, e in zip(loc["dec"], sts, loc["kst"], es)]
        for vh in range(GDN_V_HEADS):
            vs = slice(vh * GDN_DV, (vh + 1) * GDN_DV)
            y_ref[rs, vs] = _head_rmsnorm(os_[vh], hw) * _silu(zz[rs, vs])
        return new

    sts = [st_ref[vh] for vh in range(GDN_V_HEADS)]
    locs = {}
    half = nc // 2
    groups = (range(0, half), range(half, nc))

    def solve_group(chunks):
        merged = {key: [v for c in chunks for v in locs[c][key]] for key in ("p2", "rhs")}
        yield from solve_stage(merged)
        for i, c in enumerate(chunks):
            locs[c]["sol"] = merged["sol"][i * GDN_V_HEADS:(i + 1) * GDN_V_HEADS]

    def local_group(chunks):
        for c in chunks:
            yield from local_stage(c)

    def state_group(chunks):
        nonlocal sts
        for c in chunks:
            sts = state_stage(c, locs.pop(c), sts)
            yield

    _interleave(local_group(groups[0]))
    _interleave(solve_group(groups[0]), local_group(groups[1]))
    _interleave(solve_group(groups[1]), state_group(groups[0]))
    _interleave(state_group(groups[1]))
    for vh in range(GDN_V_HEADS):
        st_ref[vh] = sts[vh]
    for pad_ref in (cq_ref, ck_ref, cv_ref):
        pad_ref[0:V7X_SUBLANES, :] = pad_ref[tb:tb + V7X_SUBLANES, :]


def _gdn(x2, norm_w, w_c, conv_c, alog_pad, dtb_pad, head_w, *, layer, bsz, seq, tb=512):
    nblk = seq // tb
    row = lambda b, s: (b * nblk + s, 0)
    block_bytes = tb * D_MODEL * 4 + tb * GDN_V_WIDTH * 4
    resident_bytes = (D_MODEL * WC_COLS * 2 + GDN_CONV_K * GDN_CONV_WIDTH * 4
                      + (GDN_V_HEADS * GDN_DK * GDN_DV + (V7X_SUBLANES + tb) * GDN_CONV_WIDTH) * 4)
    temp_bytes = 6 * tb * WC_COLS * 4
    return pl.pallas_call(
        functools.partial(_gdn_kernel, layer=layer, tb=tb),
        grid=(bsz, nblk),
        in_specs=[
            pl.BlockSpec((tb, D_MODEL), row),
            _layer_rows(D_MODEL),
            _layer_slab(layer, (D_MODEL, WC_COLS)),
            _layer_slab(layer, (GDN_CONV_K, GDN_CONV_WIDTH)),
            _layer_rows(SMALL_COLS),
            _layer_rows(SMALL_COLS),
            _layer_rows(GDN_DV),
        ],
        out_specs=pl.BlockSpec((tb, GDN_V_WIDTH), row),
        out_shape=jax.ShapeDtypeStruct((bsz * seq, GDN_V_WIDTH), F32),
        scratch_shapes=[
            pltpu.VMEM((GDN_V_HEADS, GDN_DK, GDN_DV), F32),
            pltpu.VMEM((V7X_SUBLANES + tb, GDN_QK_WIDTH), F32),
            pltpu.VMEM((V7X_SUBLANES + tb, GDN_QK_WIDTH), F32),
            pltpu.VMEM((V7X_SUBLANES + tb, GDN_V_WIDTH), F32),
        ],
        compiler_params=pltpu.CompilerParams(
            dimension_semantics=("arbitrary", "arbitrary"),
            vmem_limit_bytes=_vmem_limit(block_bytes, resident_bytes, temp_bytes)),
        name="gdn",
    )(x2, norm_w, w_c, conv_c, alog_pad, dtb_pad, head_w)


def _hgrn_merge_kernel(x_ref, nw_ref, wb_ref, lb_ref, hw_ref, wm_ref, yc_ref, ca_ref, bg_ref,
                       wa_ref, wbo_ref, wc_ref, wo_ref, fw_ref,
                       out_ref, st_ref, pad_ref, yb_ref, *, layer, final, tb):
    @pl.when(pl.program_id(1) == 0)
    def _():
        st_ref[...] = jnp.zeros_like(st_ref)
        pad_ref[0:V7X_SUBLANES, :] = jnp.zeros((V7X_SUBLANES, CONV_WIDTH), F32)

    x = x_ref[...]
    h = _rmsnorm_bf16(x, nw_ref[layer:layer + 1, :])
    seg = lambda i: _proj(h, wm_ref, WM_A + i * CONV_WIDTH, CONV_WIDTH)
    bg = bg_ref[layer:layer + 1, :]
    gate = lambda i: _sigmoid(_proj(h, wm_ref, WM_G + i * D_MODEL, D_MODEL) + bg[:, i * D_MODEL:(i + 1) * D_MODEL])
    part = {}

    def conv_a():
        part["conv"] = _causal_conv(seg(1) * seg(2), pad_ref, ca_ref[...], CONV_K)

    def branch_a():
        ya = seg(0) * part["conv"] * _silu(seg(3))
        part["ya2"] = jnp.dot(ya.astype(BF16), wa_ref[...], preferred_element_type=F32)

    def branch_c():
        part["yc2"] = jnp.dot(yc_ref[...].astype(BF16), wc_ref[...], preferred_element_type=F32)

    def gate_a():
        part["merged"] = gate(0) * part["ya2"]

    def gate_c():
        part["merged"] = part["merged"] + gate(2) * part["yc2"]

    def gate_b():
        part["gate_b"] = gate(1)

    _hgrn_block(h, wb_ref, lb_ref, hw_ref, st_ref, yb_ref, layer=layer, tb=tb,
                between=(None, conv_a, branch_a, branch_c, None, gate_a, None, gate_c, gate_b))

    yb2 = jnp.dot(yb_ref[...].astype(BF16), wbo_ref[...], preferred_element_type=F32)
    merged = part["merged"] + part["gate_b"] * yb2
    out = x + jnp.dot(merged.astype(BF16), wo_ref[...], preferred_element_type=F32)
    if final:
        ms = jnp.mean(out * out, axis=-1, keepdims=True)
        out = out * lax.rsqrt(ms + NORM_EPS) * fw_ref[...]
    out_ref[...] = out


def _hgrn_merge(x2, norm_w, w_b, lower_bounds, head_w, w_m, yc, conv_a, b_gate, wa, wb, wc, wo, final_w,
                *, layer, final, bsz, seq, tb=512):
    nblk = seq // tb
    row = lambda b, s: (b * nblk + s, 0)
    block_bytes = tb * (2 * D_MODEL + GDN_V_WIDTH) * 4
    resident_bytes = ((D_MODEL * (WB_COLS + WM_COLS) + (CONV_WIDTH + HGRN_WIDTH + GDN_V_WIDTH + D_MODEL) * D_MODEL) * 2
                      + (HGRN_HEADS * HGRN_DK * HGRN_DK + (V7X_SUBLANES + tb) * CONV_WIDTH + tb * HGRN_WIDTH) * 4)
    temp_bytes = 4 * tb * (WB_COLS + WM_COLS) * 4
    return pl.pallas_call(
        functools.partial(_hgrn_merge_kernel, layer=layer, final=final, tb=tb),
        grid=(bsz, nblk),
        in_specs=[
            pl.BlockSpec((tb, D_MODEL), row),
            _layer_rows(D_MODEL),
            _layer_slab(layer, (D_MODEL, WB_COLS)),
            _resident((DEPTH, HGRN_WIDTH)),
            _layer_rows(HGRN_DK),
            _layer_slab(layer, (D_MODEL, WM_COLS)),
            pl.BlockSpec((tb, GDN_V_WIDTH), row),
            _layer_slab(layer, (CONV_K, CONV_WIDTH)),
            _layer_rows(3 * D_MODEL),
            _layer_slab(layer, (CONV_WIDTH, D_MODEL)),
            _layer_slab(layer, (HGRN_WIDTH, D_MODEL)),
            _layer_slab(layer, (GDN_V_WIDTH, D_MODEL)),
            _layer_slab(layer, (D_MODEL, D_MODEL)),
            _resident((1, D_MODEL)),
        ],
        out_specs=pl.BlockSpec((tb, D_MODEL), row),
        out_shape=jax.ShapeDtypeStruct((bsz * seq, D_MODEL), F32),
        scratch_shapes=[
            pltpu.VMEM((HGRN_HEADS, HGRN_DK, HGRN_DK), F32),
            pltpu.VMEM((V7X_SUBLANES + tb, CONV_WIDTH), F32),
            pltpu.VMEM((tb, HGRN_WIDTH), F32),
        ],
        compiler_params=pltpu.CompilerParams(
            dimension_semantics=("arbitrary", "arbitrary"),
            vmem_limit_bytes=_vmem_limit(block_bytes, resident_bytes, temp_bytes)),
        name="hgrn_merge_final" if final else "hgrn_merge",
    )(x2, norm_w, w_b, lower_bounds, head_w, w_m, yc, conv_a, b_gate, wa, wb, wc, wo, final_w)


PREP_ROWS = 128
PREP_COLS = 128
PREP_NB, PREP_NC, PREP_NM = WB_COLS // PREP_COLS, WC_COLS // PREP_COLS, WM_COLS // PREP_COLS


def _prep_sources():
    span = lambda start, width: [start + i * PREP_COLS for i in range(width // PREP_COLS)]
    return (span(SRC_B, WB_COLS)
            + span(SRC_CQKV, GDN_CONV_WIDTH) + span(SRC_CZ, GDN_V_WIDTH) + [SRC_SMALL]
            + span(SRC_A, 4 * CONV_WIDTH) + span(SRC_G, 3 * D_MODEL))


def _prep_in_kernel(src_ref, wt_ref, pb_ref, pc_ref, pm_ref):
    del src_ref
    j = pl.program_id(1)
    lane = lax.broadcasted_iota(jnp.int32, (D_MODEL, PREP_COLS), 1)
    pad = jnp.logical_and(j == PREP_NB + PREP_NC - 1, lane >= 2 * GDN_V_HEADS)
    blk = jnp.where(pad, 0.0, wt_ref[0].T).astype(BF16)

    @pl.when(j < PREP_NB)
    def _():
        pb_ref[...] = blk

    @pl.when(jnp.logical_and(j >= PREP_NB, j < PREP_NB + PREP_NC))
    def _():
        pc_ref[...] = blk

    @pl.when(j >= PREP_NB + PREP_NC)
    def _():
        pm_ref[...] = blk


def _prep_in_weights(w_in):
    w_t = jnp.swapaxes(w_in, 1, 2)
    src = jnp.asarray([c // V7X_SUBLANES for c in _prep_sources()], jnp.int32)
    out_spec = lambda first, count: pl.BlockSpec(
        (None, D_MODEL, PREP_COLS), lambda l, j, src_ref: (l, 0, jnp.clip(j - first, 0, count - 1)))
    out = lambda cols: jax.ShapeDtypeStruct((DEPTH, D_MODEL, cols), BF16)
    block_bytes = PREP_COLS * D_MODEL * (4 + 3 * 2)
    return pl.pallas_call(
        _prep_in_kernel,
        grid_spec=pltpu.PrefetchScalarGridSpec(
            num_scalar_prefetch=1,
            grid=(DEPTH, PREP_NB + PREP_NC + PREP_NM),
            in_specs=[pl.BlockSpec((pl.Element(1), pl.Element(PREP_COLS), pl.Element(D_MODEL)),
                                   lambda l, j, src_ref: (l, src_ref[j] * V7X_SUBLANES, 0))],
            out_specs=[out_spec(0, PREP_NB), out_spec(PREP_NB, PREP_NC), out_spec(PREP_NB + PREP_NC, PREP_NM)],
        ),
        out_shape=[out(WB_COLS), out(WC_COLS), out(WM_COLS)],
        compiler_params=pltpu.CompilerParams(
            dimension_semantics=("arbitrary", "arbitrary"),
            vmem_limit_bytes=_vmem_limit(block_bytes, 0, 4 * PREP_COLS * D_MODEL * 4)),
        name="prep_in_weights",
    )(src, w_t)


def _prep_out_kernel(wa_ref, wb_ref, wc_ref, wo_ref, oa_ref, ob_ref, oc_ref, oo_ref):
    oa_ref[...] = wa_ref[...].astype(BF16)
    ob_ref[...] = wb_ref[...].astype(BF16)
    oc_ref[...] = wc_ref[...].astype(BF16)
    oo_ref[...] = wo_ref[...].astype(BF16)


def _prep_out_weights(w_out_a, w_out_b, w_out_c, w_o):
    steps = D_MODEL // PREP_ROWS
    rows = (CONV_WIDTH, HGRN_WIDTH, GDN_V_WIDTH, D_MODEL)
    slab = lambda r: pl.BlockSpec((None, r // steps, D_MODEL), lambda l, i: (l, i, 0))
    return pl.pallas_call(
        _prep_out_kernel,
        grid=(DEPTH, steps),
        in_specs=[slab(r) for r in rows],
        out_specs=[slab(r) for r in rows],
        out_shape=[jax.ShapeDtypeStruct((DEPTH, r, D_MODEL), BF16) for r in rows],
        compiler_params=pltpu.CompilerParams(
            dimension_semantics=("arbitrary", "arbitrary"),
            vmem_limit_bytes=_vmem_limit(sum(rows) // steps * D_MODEL * 6, 0, 0)),
        name="prep_out_weights",
    )(w_out_a, w_out_b, w_out_c, w_o)


def kernel(x, norm_w, w_in, b_gate, conv_a, conv_c, a_log, dt_bias, lower_bounds, hgrn_norm_w, gdn_norm_w,
           w_out_a, w_out_b, w_out_c, w_o, final_norm_w):
    bsz, seq, _ = x.shape
    x2 = x.reshape(bsz * seq, D_MODEL)
    w_b, w_c, w_m = _prep_in_weights(w_in)
    wa, wb, wc, wo = _prep_out_weights(w_out_a, w_out_b, w_out_c, w_o)
    pad8 = lambda a: jnp.pad(a.astype(F32), ((0, 0), (GDN_V_HEADS, SMALL_COLS - 2 * GDN_V_HEADS)))
    alog_pad, dtb_pad = pad8(a_log), pad8(dt_bias)
    final_w = final_norm_w[None, :]
    for l in range(DEPTH):
        yc = _gdn(x2, norm_w, w_c, conv_c, alog_pad, dtb_pad, gdn_norm_w, layer=l, bsz=bsz, seq=seq)
        x2 = _hgrn_merge(x2, norm_w, w_b, lower_bounds, hgrn_norm_w, w_m, yc, conv_a, b_gate, wa, wb, wc, wo,
                         final_w, layer=l, final=(l == DEPTH - 1), bsz=bsz, seq=seq)
    return x2.reshape(bsz, seq, D_MODEL)
```

```python
import functools

import jax
import jax.numpy as jnp
from jax import lax
from jax.experimental import pallas as pl
from jax.experimental.pallas import tpu as pltpu

F32 = jnp.float32
BF16 = jnp.bfloat16

D_MODEL = 1024
DEPTH = 2
CHUNK = 64
NORM_EPS = 1e-6
L2_EPS = 1e-6
MIN_F = 1e-30
CONV_WIDTH = 512
CONV_K = 3
HGRN_HEADS = 4
HGRN_DK = 128
HGRN_WIDTH = HGRN_HEADS * HGRN_DK
GDN_QK_HEADS = 4
GDN_V_HEADS = 8
GDN_DK = 128
GDN_DV = 128
GDN_CONV_K = 4
GDN_QK_WIDTH = GDN_QK_HEADS * GDN_DK
GDN_V_WIDTH = GDN_V_HEADS * GDN_DV
GDN_CONV_WIDTH = 2 * GDN_QK_WIDTH + GDN_V_WIDTH

SRC_A = 0
SRC_B = SRC_A + 4 * CONV_WIDTH
SRC_CQKV = SRC_B + 4 * HGRN_WIDTH
SRC_SMALL = SRC_CQKV + GDN_CONV_WIDTH
SRC_CZ = SRC_SMALL + 2 * GDN_V_HEADS
SRC_G = SRC_CZ + GDN_V_WIDTH
SMALL_COLS = 128

WB_COLS = 4 * HGRN_WIDTH
WC_Q, WC_K, WC_V, WC_Z, WC_SMALL = 0, 512, 1024, 2048, 3072
WC_COLS = WC_SMALL + SMALL_COLS
WM_A, WM_G = 0, 4 * CONV_WIDTH
WM_COLS = WM_G + 3 * D_MODEL

V7X_SUBLANES = 8
V7X_VMEM_BYTES = 64 * 1024 * 1024


def _vmem_limit(block_bytes, resident_bytes, temp_bytes):
    need = 2 * block_bytes + resident_bytes + temp_bytes
    return int(min(V7X_VMEM_BYTES - 8 * 1024 * 1024, need + need // 4 + 4 * 1024 * 1024))


def _resident(shape):
    return pl.BlockSpec(shape, lambda b, s: (0,) * len(shape), pipeline_mode=pl.Buffered(1))


def _layer_rows(width):
    return _resident((DEPTH, width))


def _layer_slab(layer, shape):
    return pl.BlockSpec((None,) + tuple(shape), lambda b, s: (layer,) + (0,) * len(shape),
                        pipeline_mode=pl.Buffered(1))


def _sigmoid(x):
    return 0.5 * jnp.tanh(0.5 * x) + 0.5


def _silu(x):
    return x * _sigmoid(x)


def _softplus(x):
    return jnp.maximum(x, 0.0) + jnp.log(1.0 + jnp.exp(-jnp.abs(x)))


def _dot(a, b):
    return jnp.dot(a.astype(BF16), b.astype(BF16), preferred_element_type=F32)


def _dot_nt(a, b):
    return lax.dot_general(a.astype(BF16), b.astype(BF16), (((1,), (1,)), ((), ())),
                           preferred_element_type=F32)


def _rmsnorm_bf16(x, w):
    ms = jnp.mean(x * x, axis=-1, keepdims=True)
    return (x * lax.rsqrt(ms + NORM_EPS) * w).astype(BF16)


def _proj(h, w_ref, start, width):
    return jnp.dot(h, w_ref[:, start:start + width], preferred_element_type=F32)


def _causal_conv(x, pad_ref, w, width):
    n = x.shape[0]
    pad_ref[V7X_SUBLANES:V7X_SUBLANES + n, :] = x
    acc = x * w[width - 1:width, :]
    for j in range(1, width):
        acc = acc + pad_ref[V7X_SUBLANES - j:V7X_SUBLANES - j + n, :] * w[width - 1 - j:width - j, :]
    pad_ref[0:V7X_SUBLANES, :] = x[n - V7X_SUBLANES:n, :]
    return acc


def _chunk_cumsum_rows(x):
    pos = lax.broadcasted_iota(jnp.int32, x.shape, 0) & (CHUNK - 1)
    d = 1
    while d < CHUNK:
        xr = pltpu.roll(x, d, 0)
        x = x + jnp.where(pos >= d, xr, 0.0)
        d *= 2
    return x


def _chunk_last_rows(x):
    n = x.shape[0]
    parts = [jnp.broadcast_to(x[c * CHUNK + CHUNK - 1:(c + 1) * CHUNK, :], (CHUNK, x.shape[1]))
             for c in range(n // CHUNK)]
    return jnp.concatenate(parts, axis=0)


def _head_rmsnorm(o, w):
    return o * lax.rsqrt(jnp.mean(o * o, axis=-1, keepdims=True) + NORM_EPS) * w


HGRN_LEVELS = (32, 16, 8, 4, 2)


def _hgrn_masks():
    n = CHUNK
    rowi = lax.broadcasted_iota(jnp.int32, (n, n), 0)
    coli = lax.broadcasted_iota(jnp.int32, (n, n), 1)
    levels = []
    for half in HGRN_LEVELS:
        blk = 2 * half
        levels.append(((rowi & -blk) == (coli & -blk)) & ((rowi & (blk - 1)) >= half) & ((coli & (blk - 1)) < half))
    row2 = lax.broadcasted_iota(jnp.int32, (n, 2 * n), 0)
    lane2 = lax.broadcasted_iota(jnp.int32, (n, 2 * n), 1)
    pair = (lane2 == row2) | ((lane2 == row2 + (n - 1)) & ((row2 & 1) == 1))
    sub = lax.broadcasted_iota(jnp.int32, (n, HGRN_DK), 0) & (V7X_SUBLANES - 1)
    return levels, pair, sub


def _level_ref_rows(g, half, sub):
    n = CHUNK
    g3 = g.reshape(n // 8, 8, HGRN_DK)
    bcast8 = lambda r: jnp.broadcast_to(g3[:, r:r + 1, :], g3.shape).reshape(n, HGRN_DK)
    if half >= 8:
        blk = 2 * half
        nb = n // blk
        return jnp.broadcast_to(g.reshape(nb, blk, HGRN_DK)[:, half:half + 1, :],
                                (nb, blk, HGRN_DK)).reshape(n, HGRN_DK)
    if half == 4:
        return bcast8(4)
    return jnp.where(sub < 4, bcast8(2), bcast8(6))


def _hgrn_block(h, w_ref, lb_ref, hw_ref, st_ref, yb_ref, *, layer, tb, between=()):
    n_slots = 4 + len(HGRN_LEVELS) + 1 + tb // CHUNK
    plan = [[] for _ in range(n_slots)]
    for i, work in enumerate(between):
        plan[i * n_slots // len(between)].append(work)

    def run_between():
        for work in plan.pop(0):
            work()

    rows = [lb_ref[i:i + 1, :] for i in range(DEPTH)]
    mx = functools.reduce(jnp.maximum, rows)
    es = [jnp.exp(r - mx) for r in rows]
    tot = functools.reduce(lambda a_, b_: a_ + b_, es)
    lb = jnp.zeros_like(tot)
    for i in range(1, layer + 1):
        lb = lb + es[i] / tot

    f_gate = lb + (1.0 - lb) * _sigmoid(_proj(h, w_ref, HGRN_WIDTH, HGRN_WIDTH))
    f_floor = jnp.maximum(f_gate, MIN_F)
    run_between()
    g_all = _chunk_cumsum_rows(jnp.log(f_floor))
    k_all = 1.0 - f_gate
    kp_all = k_all * pltpu.roll(f_floor, tb - 1, 0)
    run_between()
    q_all = _silu(_proj(h, w_ref, 0, HGRN_WIDTH)) * (HGRN_DK ** -0.5)
    v_all = _proj(h, w_ref, 2 * HGRN_WIDTH, HGRN_WIDTH)
    run_between()
    g_last_all = _chunk_last_rows(g_all)
    qi_all = q_all * jnp.exp(g_all)
    ks_all = k_all * jnp.exp(g_last_all - g_all)
    dec_all = jnp.exp(g_last_all)
    run_between()

    n = CHUNK
    nc = tb // CHUNK
    level_masks, pair_mask, sub = _hgrn_masks()
    tiles = [(slice(c * CHUNK, (c + 1) * CHUNK), slice(hd * HGRN_DK, (hd + 1) * HGRN_DK))
             for hd in range(HGRN_HEADS) for c in range(nc)]
    qs = [q_all[t] for t in tiles]
    ks = [k_all[t] for t in tiles]
    gs = [g_all[t] for t in tiles]
    vs = [v_all[t] for t in tiles]
    scores = [None] * len(tiles)
    for half, mask in zip(HGRN_LEVELS, level_masks):
        es = [jnp.exp(-jnp.abs(g - _level_ref_rows(g, half, sub))) for g in gs]
        ss = [_dot_nt(q * e, k * e) for q, k, e in zip(qs, ks, es)]
        scores = [jnp.where(mask, s, 0.0 if a is None else a) for s, a in zip(ss, scores)]
        run_between()
    s01 = [jnp.where(pair_mask, _dot_nt(q, jnp.concatenate([k, kp_all[t]], axis=0)), 0.0)
           for q, k, t in zip(qs, ks, tiles)]
    scores = [a + (s + pltpu.roll(s, n, 1))[:, :n] for a, s in zip(scores, s01)]
    intra = [_dot(a, v) for a, v in zip(scores, vs)]
    upd = [_dot(v.T, ks_all[t]) for v, t in zip(vs, tiles)]
    run_between()

    zz = _proj(h, w_ref, 3 * HGRN_WIDTH, HGRN_WIDTH)
    hw = hw_ref[layer:layer + 1, :]
    sts = [st_ref[hd] for hd in range(HGRN_HEADS)]
    for c in range(nc):
        for hd in range(HGRN_HEADS):
            i = hd * nc + c
            rs, cs = tiles[i]
            o = intra[i] + _dot_nt(qi_all[rs, cs], sts[hd])
            sts[hd] = dec_all[c * CHUNK:c * CHUNK + 1, cs] * sts[hd] + upd[i]
            yb_ref[rs, cs] = _head_rmsnorm(o, hw) * _silu(zz[rs, cs])
        run_between()
    for hd in range(HGRN_HEADS):
        st_ref[hd] = sts[hd]
    assert not plan


def _l2norm_heads(x, heads, width):
    outs = []
    for hd in range(heads):
        xh = x[:, hd * width:(hd + 1) * width]
        outs.append(xh * lax.rsqrt(jnp.sum(xh * xh, axis=-1, keepdims=True) + L2_EPS))
    return outs


def _unit_lower_inverse_rows(p2, masks):
    upper, blk16, blk32, eye_lower, eye_upper = masks
    n = CHUNK
    zero_eye = eye_upper.astype(BF16)
    cs = [jnp.where(blk16 & ~upper, p, 0.0) + eye_upper for p in p2]
    for _ in range(4):
        cbs = [c.astype(BF16) for c in cs]
        cs = [jnp.dot(cb, jnp.concatenate([cb, zero_eye], axis=0), preferred_element_type=F32) for cb in cbs]
        yield
    for off in (blk32 & ~blk16, ~blk32):
        ys = [_dot(jnp.where(off, p, 0.0)[:, :n], c) for p, c in zip(p2, cs)]
        yield
        cs = [_dot(c + eye_lower, jnp.concatenate([c, y], axis=0)) for c, y in zip(cs, ys)]
        yield
    return cs


def _interleave(*gens):
    live = list(gens)
    while live:
        for g in list(live):
            try:
                next(g)
            except StopIteration:
                live.remove(g)


def _conv_rows(pad_ref, w, rows, width):
    base = V7X_SUBLANES + rows.start
    n = rows.stop - rows.start
    acc = pad_ref[base:base + n, :] * w[width - 1:width, :]
    for j in range(1, width):
        acc = acc + pad_ref[base - j:base - j + n, :] * w[width - 1 - j:width - j, :]
    return acc


def _gdn_kernel(x_ref, nw_ref, w_ref, cw_ref, alog_ref, dtb_ref, hw_ref,
                y_ref, st_ref, cq_ref, ck_ref, cv_ref, *, layer, tb):
    @pl.when(pl.program_id(1) == 0)
    def _():
        st_ref[...] = jnp.zeros_like(st_ref)
        for pad_ref in (cq_ref, ck_ref, cv_ref):
            pad_ref[0:V7X_SUBLANES, :] = jnp.zeros((V7X_SUBLANES, pad_ref.shape[1]), F32)

    n = CHUNK
    nc = tb // CHUNK
    rep = GDN_V_HEADS // GDN_QK_HEADS
    h = _rmsnorm_bf16(x_ref[...], nw_ref[layer:layer + 1, :])
    cw = cw_ref[...]
    cq_ref[V7X_SUBLANES:, :] = _proj(h, w_ref, WC_Q, GDN_QK_WIDTH)
    ck_ref[V7X_SUBLANES:, :] = _proj(h, w_ref, WC_K, GDN_QK_WIDTH)
    cv_ref[V7X_SUBLANES:, :] = _proj(h, w_ref, WC_V, GDN_V_WIDTH)
    sm = _proj(h, w_ref, WC_SMALL, SMALL_COLS)
    zz = _proj(h, w_ref, WC_Z, GDN_V_WIDTH)
    alog = alog_ref[layer:layer + 1, :]
    dtb = dtb_ref[layer:layer + 1, :]
    hw = hw_ref[layer:layer + 1, :]

    row2 = lax.broadcasted_iota(jnp.int32, (n, 2 * n), 0)
    lane2 = lax.broadcasted_iota(jnp.int32, (n, 2 * n), 1)
    col2 = lane2 & (n - 1)
    upper = lane2 >= n
    strict2 = row2 > col2
    causal = lax.broadcasted_iota(jnp.int32, (n, n), 0) >= lax.broadcasted_iota(jnp.int32, (n, n), 1)
    blk16 = (row2 >> 4) == (col2 >> 4)
    blk32 = (row2 >> 5) == (col2 >> 5)
    eye_lower = jnp.where((row2 == col2) & ~upper, 1.0, 0.0).astype(F32)
    eye_upper = jnp.where((row2 == col2) & upper, 1.0, 0.0).astype(F32)
    masks = (upper, blk16, blk32, eye_lower, eye_upper)
    lane = lax.broadcasted_iota(jnp.int32, (n, SMALL_COLS), 1)
    zeros_rhs = jnp.zeros((n, GDN_DV + GDN_DK), F32)

    def local_stage(c):
        rs = slice(c * CHUNK, (c + 1) * CHUNK)
        qc = _silu(_conv_rows(cq_ref, cw[:, 0:GDN_QK_WIDTH], rs, GDN_CONV_K))
        kc = _silu(_conv_rows(ck_ref, cw[:, GDN_QK_WIDTH:2 * GDN_QK_WIDTH], rs, GDN_CONV_K))
        vc = _silu(_conv_rows(cv_ref, cw[:, 2 * GDN_QK_WIDTH:], rs, GDN_CONV_K))
        qs = [x * (GDN_DK ** -0.5) for x in _l2norm_heads(qc, GDN_QK_HEADS, GDN_DK)]
        ks = _l2norm_heads(kc, GDN_QK_HEADS, GDN_DK)
        smc = sm[rs]
        beta = _sigmoid(smc)
        g = _chunk_cumsum_rows(-jnp.exp(alog) * _softplus(smc + dtb))
        comb_t = jnp.where(lane < GDN_V_HEADS, beta, g).T
        g_last = g[n - 1:n, :]
        eg_all = jnp.exp(g)
        kfac_all = pltpu.roll(beta, GDN_V_HEADS, 1) * jnp.exp(g_last - g)
        dec_all = jnp.exp(g_last)
        out = dict(p2=[], aqk=[], rhs=[], qi=[], kst=[], dec=[])
        yield
        for hq in range(GDN_QK_HEADS):
            qh, kh = qs[hq], ks[hq]
            kq2 = _dot_nt(jnp.concatenate([-kh, qh], axis=0), jnp.concatenate([kh, kh], axis=0))
            nkk2 = kq2[:n]
            qk = kq2[n:, :n]
            for r in range(rep):
                vh = hq * rep + r
                gl = GDN_V_HEADS + vh
                g_row = comb_t[gl:gl + 1, :]
                b_row = comb_t[vh:vh + 1, :]
                g_row2 = jnp.concatenate([g_row, g_row], axis=1)
                b_row2 = jnp.concatenate([b_row, b_row], axis=1)
                dec2 = jnp.exp(jnp.minimum(g[:, gl:gl + 1] - g_row2, 0.0)) * b_row2
                eg = eg_all[:, gl:gl + 1]
                out["p2"].append(jnp.where(strict2, nkk2 * dec2, 0.0))
                out["aqk"].append(jnp.where(causal, qk * dec2[:, :n], 0.0))
                out["rhs"].append(jnp.concatenate([vc[:, vh * GDN_DV:(vh + 1) * GDN_DV], kh * eg], axis=1))
                out["qi"].append(qh * eg)
                out["kst"].append(kh * kfac_all[:, gl:gl + 1])
                out["dec"].append(dec_all[:, gl:gl + 1])
            yield
        locs[c] = out

    def solve_stage(loc):
        cs = yield from _unit_lower_inverse_rows(loc["p2"], masks)
        loc["sol"] = [_dot(c_, jnp.concatenate([zeros_rhs, rhs], axis=0)) for c_, rhs in zip(cs, loc["rhs"])]

    def state_stage(c, loc, sts):
        rs = slice(c * CHUNK, (c + 1) * CHUNK)
        wq = [_dot(jnp.concatenate([sol[:, GDN_DV:], qi], axis=0), st) for sol, qi, st in zip(loc["sol"], loc["qi"], sts)]
        es = [sol[:, :GDN_DV] - r[:n] for sol, r in zip(loc["sol"], wq)]
        ae = [_dot(jnp.concatenate([aqk, kst.T], axis=0), e) for aqk, kst, e in zip(loc["aqk"], loc["kst"], es)]
        os_ = [r[n:] + a[:n] for r, a in zip(wq, ae)]
        new = [dec * st + a[n:] for dec, st, a in zip(loc["dec"], sts, ae)]
        for vh in range(GDN_V_HEADS):
            vs = slice(vh * GDN_DV, (vh + 1) * GDN_DV)
            y_ref[rs, vs] = _head_rmsnorm(os_[vh], hw) * _silu(zz[rs, vs])
        return new

    sts = [st_ref[vh] for vh in range(GDN_V_HEADS)]
    locs = {}
    half = nc // 2
    groups = (range(0, half), range(half, nc))

    def solve_group(chunks):
        merged = {key: [v for c in chunks for v in locs[c][key]] for key in ("p2", "rhs")}
        yield from solve_stage(merged)
        for i, c in enumerate(chunks):
            locs[c]["sol"] = merged["sol"][i * GDN_V_HEADS:(i + 1) * GDN_V_HEADS]

    def local_group(chunks):
        for c in chunks:
            yield from local_stage(c)

    def state_group(chunks):
        nonlocal sts
        for c in chunks:
            sts = state_stage(c, locs.pop(c), sts)
            yield

    _interleave(local_group(groups[0]))
    _interleave(solve_group(groups[0]), local_group(groups[1]))
    _interleave(solve_group(groups[1]), state_group(groups[0]))
    _interleave(state_group(groups[1]))
    for vh in range(GDN_V_HEADS):
        st_ref[vh] = sts[vh]
    for pad_ref in (cq_ref, ck_ref, cv_ref):
        pad_ref[0:V7X_SUBLANES, :] = pad_ref[tb:tb + V7X_SUBLANES, :]


def _gdn(x2, norm_w, w_c, conv_c, alog_pad, dtb_pad, head_w, *, layer, bsz, seq, tb=512):
    nblk = seq // tb
    row = lambda b, s: (b * nblk + s, 0)
    block_bytes = tb * D_MODEL * 4 + tb * GDN_V_WIDTH * 4
    resident_bytes = (D_MODEL * WC_COLS * 2 + GDN_CONV_K * GDN_CONV_WIDTH * 4
                      + (GDN_V_HEADS * GDN_DK * GDN_DV + (V7X_SUBLANES + tb) * GDN_CONV_WIDTH) * 4)
    temp_bytes = 6 * tb * WC_COLS * 4
    return pl.pallas_call(
        functools.partial(_gdn_kernel, layer=layer, tb=tb),
        grid=(bsz, nblk),
        in_specs=[
            pl.BlockSpec((tb, D_MODEL), row),
            _layer_rows(D_MODEL),
            _layer_slab(layer, (D_MODEL, WC_COLS)),
            _layer_slab(layer, (GDN_CONV_K, GDN_CONV_WIDTH)),
            _layer_rows(SMALL_COLS),
            _layer_rows(SMALL_COLS),
            _layer_rows(GDN_DV),
        ],
        out_specs=pl.BlockSpec((tb, GDN_V_WIDTH), row),
        out_shape=jax.ShapeDtypeStruct((bsz * seq, GDN_V_WIDTH), F32),
        scratch_shapes=[
            pltpu.VMEM((GDN_V_HEADS, GDN_DK, GDN_DV), F32),
            pltpu.VMEM((V7X_SUBLANES + tb, GDN_QK_WIDTH), F32),
            pltpu.VMEM((V7X_SUBLANES + tb, GDN_QK_WIDTH), F32),
            pltpu.VMEM((V7X_SUBLANES + tb, GDN_V_WIDTH), F32),
        ],
        compiler_params=pltpu.CompilerParams(
            dimension_semantics=("arbitrary", "arbitrary"),
            vmem_limit_bytes=_vmem_limit(block_bytes, resident_bytes, temp_bytes)),
        name="gdn",
    )(x2, norm_w, w_c, conv_c, alog_pad, dtb_pad, head_w)


def _hgrn_merge_kernel(x_ref, nw_ref, wb_ref, lb_ref, hw_ref, wm_ref, yc_ref, ca_ref, bg_ref,
                       wa_ref, wbo_ref, wc_ref, wo_ref, fw_ref,
                       out_ref, st_ref, pad_ref, yb_ref, *, layer, final, tb):
    @pl.when(pl.program_id(1) == 0)
    def _():
        st_ref[...] = jnp.zeros_like(st_ref)
        pad_ref[0:V7X_SUBLANES, :] = jnp.zeros((V7X_SUBLANES, CONV_WIDTH), F32)

    x = x_ref[...]
    h = _rmsnorm_bf16(x, nw_ref[layer:layer + 1, :])
    seg = lambda i: _proj(h, wm_ref, WM_A + i * CONV_WIDTH, CONV_WIDTH)
    bg = bg_ref[layer:layer + 1, :]
    gate = lambda i: _sigmoid(_proj(h, wm_ref, WM_G + i * D_MODEL, D_MODEL) + bg[:, i * D_MODEL:(i + 1) * D_MODEL])
    part = {}

    def conv_a():
        part["conv"] = _causal_conv(seg(1) * seg(2), pad_ref, ca_ref[...], CONV_K)

    def act_a():
        part["ya"] = (seg(0) * part["conv"] * _silu(seg(3))).astype(BF16)

    def branch_a():
        part["ya2"] = jnp.dot(part["ya"], wa_ref[...], preferred_element_type=F32)

    def branch_c():
        part["yc2"] = jnp.dot(yc_ref[...].astype(BF16), wc_ref[...], preferred_element_type=F32)

    def gate_a():
        part["merged"] = gate(0) * part["ya2"]

    def gate_c():
        part["merged"] = part["merged"] + gate(2) * part["yc2"]

    def gate_b():
        part["gate_b"] = gate(1)

    _hgrn_block(h, wb_ref, lb_ref, hw_ref, st_ref, yb_ref, layer=layer, tb=tb,
                between=(conv_a, act_a, branch_a, branch_c, gate_a, gate_c, gate_b))

    yb2 = jnp.dot(yb_ref[...].astype(BF16), wbo_ref[...], preferred_element_type=F32)
    merged = part["merged"] + part["gate_b"] * yb2
    out = x + jnp.dot(merged.astype(BF16), wo_ref[...], preferred_element_type=F32)
    if final:
        ms = jnp.mean(out * out, axis=-1, keepdims=True)
        out = out * lax.rsqrt(ms + NORM_EPS) * fw_ref[...]
    out_ref[...] = out


def _hgrn_merge(x2, norm_w, w_b, lower_bounds, head_w, w_m, yc, conv_a, b_gate, wa, wb, wc, wo, final_w,
                *, layer, final, bsz, seq, tb=512):
    nblk = seq // tb
    row = lambda b, s: (b * nblk + s, 0)
    block_bytes = tb * (2 * D_MODEL + GDN_V_WIDTH) * 4
    resident_bytes = ((D_MODEL * (WB_COLS + WM_COLS) + (CONV_WIDTH + HGRN_WIDTH + GDN_V_WIDTH + D_MODEL) * D_MODEL) * 2
                      + (HGRN_HEADS * HGRN_DK * HGRN_DK + (V7X_SUBLANES + tb) * CONV_WIDTH + tb * HGRN_WIDTH) * 4)
    temp_bytes = 4 * tb * (WB_COLS + WM_COLS) * 4
    return pl.pallas_call(
        functools.partial(_hgrn_merge_kernel, layer=layer, final=final, tb=tb),
        grid=(bsz, nblk),
        in_specs=[
            pl.BlockSpec((tb, D_MODEL), row),
            _layer_rows(D_MODEL),
            _layer_slab(layer, (D_MODEL, WB_COLS)),
            _resident((DEPTH, HGRN_WIDTH)),
            _layer_rows(HGRN_DK),
            _layer_slab(layer, (D_MODEL, WM_COLS)),
            pl.BlockSpec((tb, GDN_V_WIDTH), row),
            _layer_slab(layer, (CONV_K, CONV_WIDTH)),
            _layer_rows(3 * D_MODEL),
            _layer_slab(layer, (CONV_WIDTH, D_MODEL)),
            _layer_slab(layer, (HGRN_WIDTH, D_MODEL)),
            _layer_slab(layer, (GDN_V_WIDTH, D_MODEL)),
            _layer_slab(layer, (D_MODEL, D_MODEL)),
            _resident((1, D_MODEL)),
        ],
        out_specs=pl.BlockSpec((tb, D_MODEL), row),
        out_shape=jax.ShapeDtypeStruct((bsz * seq, D_MODEL), F32),
        scratch_shapes=[
            pltpu.VMEM((HGRN_HEADS, HGRN_DK, HGRN_DK), F32),
            pltpu.VMEM((V7X_SUBLANES + tb, CONV_WIDTH), F32),
            pltpu.VMEM((tb, HGRN_WIDTH), F32),
        ],
        compiler_params=pltpu.CompilerParams(
            dimension_semantics=("arbitrary", "arbitrary"),
            vmem_limit_bytes=_vmem_limit(block_bytes, resident_bytes, temp_bytes)),
        name="hgrn_merge_final" if final else "hgrn_merge",
    )(x2, norm_w, w_b, lower_bounds, head_w, w_m, yc, conv_a, b_gate, wa, wb, wc, wo, final_w)


PREP_ROWS = 128
PREP_COLS = 512
PREP_NB, PREP_NC, PREP_NM = WB_COLS // PREP_COLS, -(-WC_COLS // PREP_COLS), WM_COLS // PREP_COLS


def _prep_sources():
    span = lambda start, width: [start + i * PREP_COLS for i in range(width // PREP_COLS)]
    return (span(SRC_B, WB_COLS)
            + span(SRC_CQKV, GDN_CONV_WIDTH) + span(SRC_CZ, GDN_V_WIDTH) + [SRC_SMALL]
            + span(SRC_A, 4 * CONV_WIDTH) + span(SRC_G, 3 * D_MODEL))


def _prep_in_kernel(src_ref, wt_ref, pb_ref, pc_ref, pm_ref):
    del src_ref
    j = pl.program_id(1)
    lane = lax.broadcasted_iota(jnp.int32, (D_MODEL, PREP_COLS), 1)
    pad = jnp.logical_and(j == PREP_NB + PREP_NC - 1, lane >= 2 * GDN_V_HEADS)
    blk = jnp.where(pad, 0.0, wt_ref[0].T).astype(BF16)

    @pl.when(j < PREP_NB)
    def _():
        pb_ref[...] = blk

    @pl.when(jnp.logical_and(j >= PREP_NB, j < PREP_NB + PREP_NC))
    def _():
        pc_ref[...] = blk

    @pl.when(j >= PREP_NB + PREP_NC)
    def _():
        pm_ref[...] = blk


def _prep_in_weights(w_in):
    w_t = jnp.swapaxes(w_in, 1, 2)
    src = jnp.asarray([c // V7X_SUBLANES for c in _prep_sources()], jnp.int32)
    out_spec = lambda first, count: pl.BlockSpec(
        (None, D_MODEL, PREP_COLS), lambda l, j, src_ref: (l, 0, jnp.clip(j - first, 0, count - 1)))
    out = lambda cols: jax.ShapeDtypeStruct((DEPTH, D_MODEL, cols), BF16)
    block_bytes = PREP_COLS * D_MODEL * (4 + 3 * 2)
    return pl.pallas_call(
        _prep_in_kernel,
        grid_spec=pltpu.PrefetchScalarGridSpec(
            num_scalar_prefetch=1,
            grid=(DEPTH, PREP_NB + PREP_NC + PREP_NM),
            in_specs=[pl.BlockSpec((pl.Element(1), pl.Element(PREP_COLS), pl.Element(D_MODEL)),
                                   lambda l, j, src_ref: (l, src_ref[j] * V7X_SUBLANES, 0))],
            out_specs=[out_spec(0, PREP_NB), out_spec(PREP_NB, PREP_NC), out_spec(PREP_NB + PREP_NC, PREP_NM)],
        ),
        out_shape=[out(WB_COLS), out(WC_COLS), out(WM_COLS)],
        compiler_params=pltpu.CompilerParams(
            dimension_semantics=("arbitrary", "arbitrary"),
            vmem_limit_bytes=_vmem_limit(block_bytes, 0, 4 * PREP_COLS * D_MODEL * 4)),
        name="prep_in_weights",
    )(src, w_t)


def _prep_out_kernel(wa_ref, wb_ref, wc_ref, wo_ref, oa_ref, ob_ref, oc_ref, oo_ref):
    oa_ref[...] = wa_ref[...].astype(BF16)
    ob_ref[...] = wb_ref[...].astype(BF16)
    oc_ref[...] = wc_ref[...].astype(BF16)
    oo_ref[...] = wo_ref[...].astype(BF16)


def _prep_out_weights(w_out_a, w_out_b, w_out_c, w_o):
    steps = D_MODEL // PREP_ROWS
    rows = (CONV_WIDTH, HGRN_WIDTH, GDN_V_WIDTH, D_MODEL)
    slab = lambda r: pl.BlockSpec((None, r // steps, D_MODEL), lambda l, i: (l, i, 0))
    return pl.pallas_call(
        _prep_out_kernel,
        grid=(DEPTH, steps),
        in_specs=[slab(r) for r in rows],
        out_specs=[slab(r) for r in rows],
        out_shape=[jax.ShapeDtypeStruct((DEPTH, r, D_MODEL), BF16) for r in rows],
        compiler_params=pltpu.CompilerParams(
            dimension_semantics=("arbitrary", "arbitrary"),
            vmem_limit_bytes=_vmem_limit(sum(rows) // steps * D_MODEL * 6, 0, 0)),
        name="prep_out_weights",
    )(w_out_a, w_out_b, w_out_c, w_o)


def kernel(x, norm_w, w_in, b_gate, conv_a, conv_c, a_log, dt_bias, lower_bounds, hgrn_norm_w, gdn_norm_w,
           w_out_a, w_out_b, w_out_c, w_o, final_norm_w):
    bsz, seq, _ = x.shape
    x2 = x.reshape(bsz * seq, D_MODEL)
    w_b, w_c, w_m = _prep_in_weights(w_in)
    wa, wb, wc, wo = _prep_out_weights(w_out_a, w_out_b, w_out_c, w_o)
    pad8 = lambda a: jnp.pad(a.astype(F32), ((0, 0), (GDN_V_HEADS, SMALL_COLS - 2 * GDN_V_HEADS)))
    alog_pad, dtb_pad = pad8(a_log), pad8(dt_bias)
    final_w = final_norm_w[None, :]
    for l in range(DEPTH):
        yc = _gdn(x2, norm_w, w_c, conv_c, alog_pad, dtb_pad, gdn_norm_w, layer=l, bsz=bsz, seq=seq)
        x2 = _hgrn_merge(x2, norm_w, w_b, lower_bounds, hgrn_norm_w, w_m, yc, conv_a, b_gate, wa, wb, wc, wo,
                         final_w, layer=l, final=(l == DEPTH - 1), bsz=bsz, seq=seq)
    return x2.reshape(bsz, seq, D_MODEL)
```

```python
import functools

import jax
import jax.numpy as jnp
from jax import lax
from jax.experimental import pallas as pl
from jax.experimental.pallas import tpu as pltpu

F32 = jnp.float32
BF16 = jnp.bfloat16

D_MODEL = 1024
DEPTH = 2
CHUNK = 64
NORM_EPS = 1e-6
L2_EPS = 1e-6
MIN_F = 1e-30
CONV_WIDTH = 512
CONV_K = 3
HGRN_HEADS = 4
HGRN_DK = 128
HGRN_WIDTH = HGRN_HEADS * HGRN_DK
GDN_QK_HEADS = 4
GDN_V_HEADS = 8
GDN_DK = 128
GDN_DV = 128
GDN_CONV_K = 4
GDN_QK_WIDTH = GDN_QK_HEADS * GDN_DK
GDN_V_WIDTH = GDN_V_HEADS * GDN_DV
GDN_CONV_WIDTH = 2 * GDN_QK_WIDTH + GDN_V_WIDTH
GDN_GROUPS = 2

SRC_A = 0
SRC_B = SRC_A + 4 * CONV_WIDTH
SRC_CQKV = SRC_B + 4 * HGRN_WIDTH
SRC_SMALL = SRC_CQKV + GDN_CONV_WIDTH
SRC_CZ = SRC_SMALL + 2 * GDN_V_HEADS
SRC_G = SRC_CZ + GDN_V_WIDTH
SMALL_COLS = 128

WB_COLS = 4 * HGRN_WIDTH
WC_Q, WC_K, WC_V, WC_Z, WC_SMALL = 0, 512, 1024, 2048, 3072
WC_COLS = WC_SMALL + SMALL_COLS
WM_A, WM_G = 0, 4 * CONV_WIDTH
WM_COLS = WM_G + 3 * D_MODEL

V7X_SUBLANES = 8
V7X_VMEM_BYTES = 64 * 1024 * 1024


def _vmem_limit(block_bytes, resident_bytes, temp_bytes):
    need = 2 * block_bytes + resident_bytes + temp_bytes
    return int(min(V7X_VMEM_BYTES - 8 * 1024 * 1024, need + need // 4 + 4 * 1024 * 1024))


def _resident(shape):
    return pl.BlockSpec(shape, lambda b, s: (0,) * len(shape), pipeline_mode=pl.Buffered(1))


def _layer_rows(width):
    return _resident((DEPTH, width))


def _layer_slab(layer, shape):
    return pl.BlockSpec((None,) + tuple(shape), lambda b, s: (layer,) + (0,) * len(shape),
                        pipeline_mode=pl.Buffered(1))


def _sigmoid(x):
    return 0.5 * jnp.tanh(0.5 * x) + 0.5


def _silu(x):
    return x * _sigmoid(x)


def _softplus(x):
    return jnp.maximum(x, 0.0) + jnp.log(1.0 + jnp.exp(-jnp.abs(x)))


def _dot(a, b):
    return jnp.dot(a.astype(BF16), b.astype(BF16), preferred_element_type=F32)


def _dot_nt(a, b):
    return lax.dot_general(a.astype(BF16), b.astype(BF16), (((1,), (1,)), ((), ())),
                           preferred_element_type=F32)


def _rmsnorm_bf16(x, w):
    ms = jnp.mean(x * x, axis=-1, keepdims=True)
    return (x * lax.rsqrt(ms + NORM_EPS) * w).astype(BF16)


def _proj(h, w_ref, start, width):
    return jnp.dot(h, w_ref[:, start:start + width], preferred_element_type=F32)


def _causal_conv(x, pad_ref, w, width):
    n = x.shape[0]
    pad_ref[V7X_SUBLANES:V7X_SUBLANES + n, :] = x
    acc = x * w[width - 1:width, :]
    for j in range(1, width):
        acc = acc + pad_ref[V7X_SUBLANES - j:V7X_SUBLANES - j + n, :] * w[width - 1 - j:width - j, :]
    pad_ref[0:V7X_SUBLANES, :] = x[n - V7X_SUBLANES:n, :]
    return acc


def _chunk_cumsum_rows(x):
    pos = lax.broadcasted_iota(jnp.int32, x.shape, 0) & (CHUNK - 1)
    d = 1
    while d < CHUNK:
        xr = pltpu.roll(x, d, 0)
        x = x + jnp.where(pos >= d, xr, 0.0)
        d *= 2
    return x


def _chunk_last_rows(x):
    n = x.shape[0]
    parts = [jnp.broadcast_to(x[c * CHUNK + CHUNK - 1:(c + 1) * CHUNK, :], (CHUNK, x.shape[1]))
             for c in range(n // CHUNK)]
    return jnp.concatenate(parts, axis=0)


def _head_rmsnorm(o, w):
    return o * lax.rsqrt(jnp.mean(o * o, axis=-1, keepdims=True) + NORM_EPS) * w


HGRN_LEVELS = (32, 16, 8, 4, 2)


def _hgrn_masks():
    n = CHUNK
    rowi = lax.broadcasted_iota(jnp.int32, (n, n), 0)
    coli = lax.broadcasted_iota(jnp.int32, (n, n), 1)
    levels = []
    for half in HGRN_LEVELS:
        blk = 2 * half
        levels.append(((rowi & -blk) == (coli & -blk)) & ((rowi & (blk - 1)) >= half) & ((coli & (blk - 1)) < half))
    row2 = lax.broadcasted_iota(jnp.int32, (n, 2 * n), 0)
    lane2 = lax.broadcasted_iota(jnp.int32, (n, 2 * n), 1)
    pair = (lane2 == row2) | ((lane2 == row2 + (n - 1)) & ((row2 & 1) == 1))
    sub = lax.broadcasted_iota(jnp.int32, (n, HGRN_DK), 0) & (V7X_SUBLANES - 1)
    return levels, pair, sub


def _level_ref_rows(g, half, sub):
    n = CHUNK
    g3 = g.reshape(n // 8, 8, HGRN_DK)
    bcast8 = lambda r: jnp.broadcast_to(g3[:, r:r + 1, :], g3.shape).reshape(n, HGRN_DK)
    if half >= 8:
        blk = 2 * half
        nb = n // blk
        return jnp.broadcast_to(g.reshape(nb, blk, HGRN_DK)[:, half:half + 1, :],
                                (nb, blk, HGRN_DK)).reshape(n, HGRN_DK)
    if half == 4:
        return bcast8(4)
    return jnp.where(sub < 4, bcast8(2), bcast8(6))


def _hgrn_block(h, w_ref, lb_ref, hw_ref, st_ref, yb_ref, *, layer, tb, between=()):
    n_slots = 4 + len(HGRN_LEVELS) + 1 + tb // CHUNK
    plan = [[] for _ in range(n_slots)]
    for i, work in enumerate(between):
        plan[i * n_slots // len(between)].append(work)

    def run_between():
        for work in plan.pop(0):
            work()

    rows = [lb_ref[i:i + 1, :] for i in range(DEPTH)]
    mx = functools.reduce(jnp.maximum, rows)
    es = [jnp.exp(r - mx) for r in rows]
    tot = functools.reduce(lambda a_, b_: a_ + b_, es)
    lb = jnp.zeros_like(tot)
    for i in range(1, layer + 1):
        lb = lb + es[i] / tot

    f_gate = lb + (1.0 - lb) * _sigmoid(_proj(h, w_ref, HGRN_WIDTH, HGRN_WIDTH))
    f_floor = jnp.maximum(f_gate, MIN_F)
    run_between()
    g_all = _chunk_cumsum_rows(jnp.log(f_floor))
    k_all = 1.0 - f_gate
    kp_all = k_all * pltpu.roll(f_floor, tb - 1, 0)
    run_between()
    q_all = _silu(_proj(h, w_ref, 0, HGRN_WIDTH)) * (HGRN_DK ** -0.5)
    v_all = _proj(h, w_ref, 2 * HGRN_WIDTH, HGRN_WIDTH)
    run_between()
    g_last_all = _chunk_last_rows(g_all)
    qi_all = q_all * jnp.exp(g_all)
    ks_all = k_all * jnp.exp(g_last_all - g_all)
    dec_all = jnp.exp(g_last_all)
    run_between()

    n = CHUNK
    nc = tb // CHUNK
    level_masks, pair_mask, sub = _hgrn_masks()
    tiles = [(slice(c * CHUNK, (c + 1) * CHUNK), slice(hd * HGRN_DK, (hd + 1) * HGRN_DK))
             for hd in range(HGRN_HEADS) for c in range(nc)]
    qs = [q_all[t] for t in tiles]
    ks = [k_all[t] for t in tiles]
    gs = [g_all[t] for t in tiles]
    vs = [v_all[t] for t in tiles]
    scores = [None] * len(tiles)
    for half, mask in zip(HGRN_LEVELS, level_masks):
        es = [jnp.exp(-jnp.abs(g - _level_ref_rows(g, half, sub))) for g in gs]
        ss = [_dot_nt(q * e, k * e) for q, k, e in zip(qs, ks, es)]
        scores = [jnp.where(mask, s, 0.0 if a is None else a) for s, a in zip(ss, scores)]
        run_between()
    s01 = [jnp.where(pair_mask, _dot_nt(q, jnp.concatenate([k, kp_all[t]], axis=0)), 0.0)
           for q, k, t in zip(qs, ks, tiles)]
    scores = [a + (s + pltpu.roll(s, n, 1))[:, :n] for a, s in zip(scores, s01)]
    intra = [_dot(a, v) for a, v in zip(scores, vs)]
    upd = [_dot(v.T, ks_all[t]) for v, t in zip(vs, tiles)]
    run_between()

    zz = _proj(h, w_ref, 3 * HGRN_WIDTH, HGRN_WIDTH)
    hw = hw_ref[layer:layer + 1, :]
    sts = [st_ref[hd] for hd in range(HGRN_HEADS)]
    for c in range(nc):
        for hd in range(HGRN_HEADS):
            i = hd * nc + c
            rs, cs = tiles[i]
            o = intra[i] + _dot_nt(qi_all[rs, cs], sts[hd])
            sts[hd] = dec_all[c * CHUNK:c * CHUNK + 1, cs] * sts[hd] + upd[i]
            yb_ref[rs, cs] = _head_rmsnorm(o, hw) * _silu(zz[rs, cs])
        run_between()
    for hd in range(HGRN_HEADS):
        st_ref[hd] = sts[hd]
    assert not plan


def _l2norm_heads(x, heads, width):
    outs = []
    for hd in range(heads):
        xh = x[:, hd * width:(hd + 1) * width]
        outs.append(xh * lax.rsqrt(jnp.sum(xh * xh, axis=-1, keepdims=True) + L2_EPS))
    return outs


def _unit_lower_inverse_rows(p2, masks):
    upper, blk16, blk32, eye_lower, eye_upper = masks
    n = CHUNK
    zero_eye = eye_upper.astype(BF16)
    cs = [jnp.where(blk16 & ~upper, p, 0.0) + eye_upper for p in p2]
    for _ in range(4):
        cbs = [c.astype(BF16) for c in cs]
        cs = [jnp.dot(cb, jnp.concatenate([cb, zero_eye], axis=0), preferred_element_type=F32) for cb in cbs]
        yield
    for off in (blk32 & ~blk16, ~blk32):
        ys = [_dot(jnp.where(off, p, 0.0)[:, :n], c) for p, c in zip(p2, cs)]
        yield
        cs = [_dot(c + eye_lower, jnp.concatenate([c, y], axis=0)) for c, y in zip(cs, ys)]
        yield
    return cs


def _interleave(*gens):
    live = list(gens)
    while live:
        for g in list(live):
            try:
                next(g)
            except StopIteration:
                live.remove(g)


def _conv_rows(pad_ref, w, rows, width):
    base = V7X_SUBLANES + rows.start
    n = rows.stop - rows.start
    acc = pad_ref[base:base + n, :] * w[width - 1:width, :]
    for j in range(1, width):
        acc = acc + pad_ref[base - j:base - j + n, :] * w[width - 1 - j:width - j, :]
    return acc


def _gdn_kernel(x_ref, nw_ref, w_ref, cw_ref, alog_ref, dtb_ref, hw_ref,
                y_ref, st_ref, cq_ref, ck_ref, cv_ref, *, layer, tb):
    @pl.when(pl.program_id(1) == 0)
    def _():
        st_ref[...] = jnp.zeros_like(st_ref)
        for pad_ref in (cq_ref, ck_ref, cv_ref):
            pad_ref[0:V7X_SUBLANES, :] = jnp.zeros((V7X_SUBLANES, pad_ref.shape[1]), F32)

    n = CHUNK
    nc = tb // CHUNK
    rep = GDN_V_HEADS // GDN_QK_HEADS
    h = _rmsnorm_bf16(x_ref[...], nw_ref[layer:layer + 1, :])
    cw = cw_ref[...]
    cq_ref[V7X_SUBLANES:, :] = _proj(h, w_ref, WC_Q, GDN_QK_WIDTH)
    ck_ref[V7X_SUBLANES:, :] = _proj(h, w_ref, WC_K, GDN_QK_WIDTH)
    cv_ref[V7X_SUBLANES:, :] = _proj(h, w_ref, WC_V, GDN_V_WIDTH)
    sm = _proj(h, w_ref, WC_SMALL, SMALL_COLS)
    zz = _proj(h, w_ref, WC_Z, GDN_V_WIDTH)
    alog = alog_ref[layer:layer + 1, :]
    dtb = dtb_ref[layer:layer + 1, :]
    hw = hw_ref[layer:layer + 1, :]

    row2 = lax.broadcasted_iota(jnp.int32, (n, 2 * n), 0)
    lane2 = lax.broadcasted_iota(jnp.int32, (n, 2 * n), 1)
    col2 = lane2 & (n - 1)
    upper = lane2 >= n
    strict2 = row2 > col2
    causal = lax.broadcasted_iota(jnp.int32, (n, n), 0) >= lax.broadcasted_iota(jnp.int32, (n, n), 1)
    blk16 = (row2 >> 4) == (col2 >> 4)
    blk32 = (row2 >> 5) == (col2 >> 5)
    eye_lower = jnp.where((row2 == col2) & ~upper, 1.0, 0.0).astype(F32)
    eye_upper = jnp.where((row2 == col2) & upper, 1.0, 0.0).astype(F32)
    masks = (upper, blk16, blk32, eye_lower, eye_upper)
    lane = lax.broadcasted_iota(jnp.int32, (n, SMALL_COLS), 1)
    zeros_rhs = jnp.zeros((n, GDN_DV + GDN_DK), F32)

    def local_stage(c):
        rs = slice(c * CHUNK, (c + 1) * CHUNK)
        qc = _silu(_conv_rows(cq_ref, cw[:, 0:GDN_QK_WIDTH], rs, GDN_CONV_K))
        kc = _silu(_conv_rows(ck_ref, cw[:, GDN_QK_WIDTH:2 * GDN_QK_WIDTH], rs, GDN_CONV_K))
        vc = _silu(_conv_rows(cv_ref, cw[:, 2 * GDN_QK_WIDTH:], rs, GDN_CONV_K))
        qs = [x * (GDN_DK ** -0.5) for x in _l2norm_heads(qc, GDN_QK_HEADS, GDN_DK)]
        ks = _l2norm_heads(kc, GDN_QK_HEADS, GDN_DK)
        smc = sm[rs]
        beta = _sigmoid(smc)
        g = _chunk_cumsum_rows(-jnp.exp(alog) * _softplus(smc + dtb))
        comb_t = jnp.where(lane < GDN_V_HEADS, beta, g).T
        g_last = g[n - 1:n, :]
        eg_all = jnp.exp(g)
        kfac_all = pltpu.roll(beta, GDN_V_HEADS, 1) * jnp.exp(g_last - g)
        dec_all = jnp.exp(g_last)
        out = dict(p2=[], aqk=[], rhs=[], qi=[], kst=[], dec=[])
        yield
        for hq in range(GDN_QK_HEADS):
            qh, kh = qs[hq], ks[hq]
            kq2 = _dot_nt(jnp.concatenate([-kh, qh], axis=0), jnp.concatenate([kh, kh], axis=0))
            nkk2 = kq2[:n]
            qk = kq2[n:, :n]
            for r in range(rep):
                vh = hq * rep + r
                gl = GDN_V_HEADS + vh
                g_row = comb_t[gl:gl + 1, :]
                b_row = comb_t[vh:vh + 1, :]
                g_row2 = jnp.concatenate([g_row, g_row], axis=1)
                b_row2 = jnp.concatenate([b_row, b_row], axis=1)
                dec2 = jnp.exp(jnp.minimum(g[:, gl:gl + 1] - g_row2, 0.0)) * b_row2
                eg = eg_all[:, gl:gl + 1]
                out["p2"].append(jnp.where(strict2, nkk2 * dec2, 0.0))
                out["aqk"].append(jnp.where(causal, qk * dec2[:, :n], 0.0))
                out["rhs"].append(jnp.concatenate([vc[:, vh * GDN_DV:(vh + 1) * GDN_DV], kh * eg], axis=1))
                out["qi"].append(qh * eg)
                out["kst"].append(kh * kfac_all[:, gl:gl + 1])
                out["dec"].append(dec_all[:, gl:gl + 1])
            yield
        locs[c] = out

    def solve_stage(loc):
        cs = yield from _unit_lower_inverse_rows(loc["p2"], masks)
        loc["sol"] = [_dot(c_, jnp.concatenate([zeros_rhs, rhs], axis=0)) for c_, rhs in zip(cs, loc["rhs"])]

    def state_stage(c, loc, sts):
        rs = slice(c * CHUNK, (c + 1) * CHUNK)
        wq = [_dot(jnp.concatenate([sol[:, GDN_DV:], qi], axis=0), st) for sol, qi, st in zip(loc["sol"], loc["qi"], sts)]
        es = [sol[:, :GDN_DV] - r[:n] for sol, r in zip(loc["sol"], wq)]
        ae = [_dot(jnp.concatenate([aqk, kst.T], axis=0), e) for aqk, kst, e in zip(loc["aqk"], loc["kst"], es)]
        os_ = [r[n:] + a[:n] for r, a in zip(wq, ae)]
        new = [dec * st + a[n:] for dec, st, a in zip(loc["dec"], sts, ae)]
        for vh in range(GDN_V_HEADS):
            vs = slice(vh * GDN_DV, (vh + 1) * GDN_DV)
            y_ref[rs, vs] = (_head_rmsnorm(os_[vh], hw) * _silu(zz[rs, vs])).astype(BF16)
        return new

    sts = [st_ref[vh] for vh in range(GDN_V_HEADS)]
    locs = {}
    per_group = nc // GDN_GROUPS
    groups = [range(i * per_group, (i + 1) * per_group) for i in range(GDN_GROUPS)]

    def solve_group(chunks):
        merged = {key: [v for c in chunks for v in locs[c][key]] for key in ("p2", "rhs")}
        yield from solve_stage(merged)
        for i, c in enumerate(chunks):
            locs[c]["sol"] = merged["sol"][i * GDN_V_HEADS:(i + 1) * GDN_V_HEADS]

    def local_group(chunks):
        for c in chunks:
            yield from local_stage(c)

    def state_group(chunks):
        nonlocal sts
        for c in chunks:
            sts = state_stage(c, locs.pop(c), sts)
            yield

    for i in range(GDN_GROUPS + 2):
        work = []
        if 0 <= i - 1 < GDN_GROUPS:
            work.append(solve_group(groups[i - 1]))
        if i < GDN_GROUPS:
            work.append(local_group(groups[i]))
        if 0 <= i - 2 < GDN_GROUPS:
            work.append(state_group(groups[i - 2]))
        _interleave(*work)
    for vh in range(GDN_V_HEADS):
        st_ref[vh] = sts[vh]
    for pad_ref in (cq_ref, ck_ref, cv_ref):
        pad_ref[0:V7X_SUBLANES, :] = pad_ref[tb:tb + V7X_SUBLANES, :]


def _gdn(x2, norm_w, w_c, conv_c, alog_pad, dtb_pad, head_w, *, layer, bsz, seq, tb=512):
    nblk = seq // tb
    row = lambda b, s: (b * nblk + s, 0)
    block_bytes = tb * D_MODEL * 4 + tb * GDN_V_WIDTH * 2
    resident_bytes = (D_MODEL * WC_COLS * 2 + GDN_CONV_K * GDN_CONV_WIDTH * 4
                      + (GDN_V_HEADS * GDN_DK * GDN_DV + (V7X_SUBLANES + tb) * GDN_CONV_WIDTH) * 4)
    temp_bytes = 6 * tb * WC_COLS * 4
    return pl.pallas_call(
        functools.partial(_gdn_kernel, layer=layer, tb=tb),
        grid=(bsz, nblk),
        in_specs=[
            pl.BlockSpec((tb, D_MODEL), row),
            _layer_rows(D_MODEL),
            _layer_slab(layer, (D_MODEL, WC_COLS)),
            _layer_slab(layer, (GDN_CONV_K, GDN_CONV_WIDTH)),
            _layer_rows(SMALL_COLS),
            _layer_rows(SMALL_COLS),
            _layer_rows(GDN_DV),
        ],
        out_specs=pl.BlockSpec((tb, GDN_V_WIDTH), row),
        out_shape=jax.ShapeDtypeStruct((bsz * seq, GDN_V_WIDTH), BF16),
        scratch_shapes=[
            pltpu.VMEM((GDN_V_HEADS, GDN_DK, GDN_DV), F32),
            pltpu.VMEM((V7X_SUBLANES + tb, GDN_QK_WIDTH), F32),
            pltpu.VMEM((V7X_SUBLANES + tb, GDN_QK_WIDTH), F32),
            pltpu.VMEM((V7X_SUBLANES + tb, GDN_V_WIDTH), F32),
        ],
        compiler_params=pltpu.CompilerParams(
            dimension_semantics=("arbitrary", "arbitrary"),
            vmem_limit_bytes=_vmem_limit(block_bytes, resident_bytes, temp_bytes)),
        name="gdn",
    )(x2, norm_w, w_c, conv_c, alog_pad, dtb_pad, head_w)


def _hgrn_merge_kernel(x_ref, nw_ref, wb_ref, lb_ref, hw_ref, wm_ref, yc_ref, ca_ref, bg_ref,
                       wa_ref, wbo_ref, wc_ref, wo_ref, fw_ref,
                       out_ref, st_ref, pad_ref, yb_ref, *, layer, final, tb):
    @pl.when(pl.program_id(1) == 0)
    def _():
        st_ref[...] = jnp.zeros_like(st_ref)
        pad_ref[0:V7X_SUBLANES, :] = jnp.zeros((V7X_SUBLANES, CONV_WIDTH), F32)

    x = x_ref[...]
    h = _rmsnorm_bf16(x, nw_ref[layer:layer + 1, :])
    seg = lambda i: _proj(h, wm_ref, WM_A + i * CONV_WIDTH, CONV_WIDTH)
    bg = bg_ref[layer:layer + 1, :]
    gate = lambda i: _sigmoid(_proj(h, wm_ref, WM_G + i * D_MODEL, D_MODEL) + bg[:, i * D_MODEL:(i + 1) * D_MODEL])
    part = {}

    def conv_a():
        part["conv"] = _causal_conv(seg(1) * seg(2), pad_ref, ca_ref[...], CONV_K)

    def act_a():
        part["ya"] = (seg(0) * part["conv"] * _silu(seg(3))).astype(BF16)

    def branch_a():
        part["ya2"] = jnp.dot(part["ya"], wa_ref[...], preferred_element_type=F32)

    def branch_c():
        part["yc2"] = jnp.dot(yc_ref[...], wc_ref[...], preferred_element_type=F32)

    def gate_a():
        part["merged"] = gate(0) * part["ya2"]

    def gate_c():
        part["merged"] = part["merged"] + gate(2) * part["yc2"]

    def gate_b():
        part["gate_b"] = gate(1)

    _hgrn_block(h, wb_ref, lb_ref, hw_ref, st_ref, yb_ref, layer=layer, tb=tb,
                between=(conv_a, act_a, branch_a, branch_c, gate_a, gate_c, gate_b))

    yb2 = jnp.dot(yb_ref[...].astype(BF16), wbo_ref[...], preferred_element_type=F32)
    merged = part["merged"] + part["gate_b"] * yb2
    out = x + jnp.dot(merged.astype(BF16), wo_ref[...], preferred_element_type=F32)
    if final:
        ms = jnp.mean(out * out, axis=-1, keepdims=True)
        out = out * lax.rsqrt(ms + NORM_EPS) * fw_ref[...]
    out_ref[...] = out


def _hgrn_merge(x2, norm_w, w_b, lower_bounds, head_w, w_m, yc, conv_a, b_gate, wa, wb, wc, wo, final_w,
                *, layer, final, bsz, seq, tb=512):
    nblk = seq // tb
    row = lambda b, s: (b * nblk + s, 0)
    block_bytes = tb * (2 * D_MODEL * 4 + GDN_V_WIDTH * 2)
    resident_bytes = ((D_MODEL * (WB_COLS + WM_COLS) + (CONV_WIDTH + HGRN_WIDTH + GDN_V_WIDTH + D_MODEL) * D_MODEL) * 2
                      + (HGRN_HEADS * HGRN_DK * HGRN_DK + (V7X_SUBLANES + tb) * CONV_WIDTH + tb * HGRN_WIDTH) * 4)
    temp_bytes = 4 * tb * (WB_COLS + WM_COLS) * 4
    return pl.pallas_call(
        functools.partial(_hgrn_merge_kernel, layer=layer, final=final, tb=tb),
        grid=(bsz, nblk),
        in_specs=[
            pl.BlockSpec((tb, D_MODEL), row),
            _layer_rows(D_MODEL),
            _layer_slab(layer, (D_MODEL, WB_COLS)),
            _resident((DEPTH, HGRN_WIDTH)),
            _layer_rows(HGRN_DK),
            _layer_slab(layer, (D_MODEL, WM_COLS)),
            pl.BlockSpec((tb, GDN_V_WIDTH), row),
            _layer_slab(layer, (CONV_K, CONV_WIDTH)),
            _layer_rows(3 * D_MODEL),
            _layer_slab(layer, (CONV_WIDTH, D_MODEL)),
            _layer_slab(layer, (HGRN_WIDTH, D_MODEL)),
            _layer_slab(layer, (GDN_V_WIDTH, D_MODEL)),
            _layer_slab(layer, (D_MODEL, D_MODEL)),
            _resident((1, D_MODEL)),
        ],
        out_specs=pl.BlockSpec((tb, D_MODEL), row),
        out_shape=jax.ShapeDtypeStruct((bsz * seq, D_MODEL), F32),
        scratch_shapes=[
            pltpu.VMEM((HGRN_HEADS, HGRN_DK, HGRN_DK), F32),
            pltpu.VMEM((V7X_SUBLANES + tb, CONV_WIDTH), F32),
            pltpu.VMEM((tb, HGRN_WIDTH), F32),
        ],
        compiler_params=pltpu.CompilerParams(
            dimension_semantics=("arbitrary", "arbitrary"),
            vmem_limit_bytes=_vmem_limit(block_bytes, resident_bytes, temp_bytes)),
        name="hgrn_merge_final" if final else "hgrn_merge",
    )(x2, norm_w, w_b, lower_bounds, head_w, w_m, yc, conv_a, b_gate, wa, wb, wc, wo, final_w)


PREP_ROWS = 128
PREP_COLS = 512
PREP_NB, PREP_NC, PREP_NM = WB_COLS // PREP_COLS, -(-WC_COLS // PREP_COLS), WM_COLS // PREP_COLS


def _prep_sources():
    span = lambda start, width: [start + i * PREP_COLS for i in range(width // PREP_COLS)]
    return (span(SRC_B, WB_COLS)
            + span(SRC_CQKV, GDN_CONV_WIDTH) + span(SRC_CZ, GDN_V_WIDTH) + [SRC_SMALL]
            + span(SRC_A, 4 * CONV_WIDTH) + span(SRC_G, 3 * D_MODEL))


def _prep_in_kernel(src_ref, wt_ref, pb_ref, pc_ref, pm_ref):
    del src_ref
    j = pl.program_id(1)
    lane = lax.broadcasted_iota(jnp.int32, (D_MODEL, PREP_COLS), 1)
    pad = jnp.logical_and(j == PREP_NB + PREP_NC - 1, lane >= 2 * GDN_V_HEADS)
    blk = jnp.where(pad, 0.0, wt_ref[0].T).astype(BF16)

    @pl.when(j < PREP_NB)
    def _():
        pb_ref[...] = blk

    @pl.when(jnp.logical_and(j >= PREP_NB, j < PREP_NB + PREP_NC))
    def _():
        pc_ref[...] = blk

    @pl.when(j >= PREP_NB + PREP_NC)
    def _():
        pm_ref[...] = blk


def _prep_in_weights(w_in):
    w_t = jnp.swapaxes(w_in, 1, 2)
    src = jnp.asarray([c // V7X_SUBLANES for c in _prep_sources()], jnp.int32)
    out_spec = lambda first, count: pl.BlockSpec(
        (None, D_MODEL, PREP_COLS), lambda l, j, src_ref: (l, 0, jnp.clip(j - first, 0, count - 1)))
    out = lambda cols: jax.ShapeDtypeStruct((DEPTH, D_MODEL, cols), BF16)
    block_bytes = PREP_COLS * D_MODEL * (4 + 3 * 2)
    return pl.pallas_call(
        _prep_in_kernel,
        grid_spec=pltpu.PrefetchScalarGridSpec(
            num_scalar_prefetch=1,
            grid=(DEPTH, PREP_NB + PREP_NC + PREP_NM),
            in_specs=[pl.BlockSpec((pl.Element(1), pl.Element(PREP_COLS), pl.Element(D_MODEL)),
                                   lambda l, j, src_ref: (l, src_ref[j] * V7X_SUBLANES, 0))],
            out_specs=[out_spec(0, PREP_NB), out_spec(PREP_NB, PREP_NC), out_spec(PREP_NB + PREP_NC, PREP_NM)],
        ),
        out_shape=[out(WB_COLS), out(WC_COLS), out(WM_COLS)],
        compiler_params=pltpu.CompilerParams(
            dimension_semantics=("arbitrary", "arbitrary"),
            vmem_limit_bytes=_vmem_limit(block_bytes, 0, 4 * PREP_COLS * D_MODEL * 4)),
        name="prep_in_weights",
    )(src, w_t)


def _prep_out_kernel(wa_ref, wb_ref, wc_ref, wo_ref, oa_ref, ob_ref, oc_ref, oo_ref):
    oa_ref[...] = wa_ref[...].astype(BF16)
    ob_ref[...] = wb_ref[...].astype(BF16)
    oc_ref[...] = wc_ref[...].astype(BF16)
    oo_ref[...] = wo_ref[...].astype(BF16)


def _prep_out_weights(w_out_a, w_out_b, w_out_c, w_o):
    steps = D_MODEL // PREP_ROWS
    rows = (CONV_WIDTH, HGRN_WIDTH, GDN_V_WIDTH, D_MODEL)
    slab = lambda r: pl.BlockSpec((None, r // steps, D_MODEL), lambda l, i: (l, i, 0))
    return pl.pallas_call(
        _prep_out_kernel,
        grid=(DEPTH, steps),
        in_specs=[slab(r) for r in rows],
        out_specs=[slab(r) for r in rows],
        out_shape=[jax.ShapeDtypeStruct((DEPTH, r, D_MODEL), BF16) for r in rows],
        compiler_params=pltpu.CompilerParams(
            dimension_semantics=("arbitrary", "arbitrary"),
            vmem_limit_bytes=_vmem_limit(sum(rows) // steps * D_MODEL * 6, 0, 0)),
        name="prep_out_weights",
    )(w_out_a, w_out_b, w_out_c, w_o)


def kernel(x, norm_w, w_in, b_gate, conv_a, conv_c, a_log, dt_bias, lower_bounds, hgrn_norm_w, gdn_norm_w,
           w_out_a, w_out_b, w_out_c, w_o, final_norm_w):
    bsz, seq, _ = x.shape
    x2 = x.reshape(bsz * seq, D_MODEL)
    w_b, w_c, w_m = _prep_in_weights(w_in)
    wa, wb, wc, wo = _prep_out_weights(w_out_a, w_out_b, w_out_c, w_o)
    pad8 = lambda a: jnp.pad(a.astype(F32), ((0, 0), (GDN_V_HEADS, SMALL_COLS - 2 * GDN_V_HEADS)))
    alog_pad, dtb_pad = pad8(a_log), pad8(dt_bias)
    final_w = final_norm_w[None, :]
    for l in range(DEPTH):
        yc = _gdn(x2, norm_w, w_c, conv_c, alog_pad, dtb_pad, gdn_norm_w, layer=l, bsz=bsz, seq=seq)
        x2 = _hgrn_merge(x2, norm_w, w_b, lower_bounds, hgrn_norm_w, w_m, yc, conv_a, b_gate, wa, wb, wc, wo,
                         final_w, layer=l, final=(l == DEPTH - 1), bsz=bsz, seq=seq)
    return x2.reshape(bsz, seq, D_MODEL)
```

```python
import functools

import jax
import jax.numpy as jnp
from jax import lax
from jax.experimental import pallas as pl
from jax.experimental.pallas import tpu as pltpu

F32 = jnp.float32
BF16 = jnp.bfloat16

D_MODEL = 1024
DEPTH = 2
CHUNK = 64
NORM_EPS = 1e-6
L2_EPS = 1e-6
MIN_F = 1e-30
CONV_WIDTH = 512
CONV_K = 3
HGRN_HEADS = 4
HGRN_DK = 128
HGRN_WIDTH = HGRN_HEADS * HGRN_DK
GDN_QK_HEADS = 4
GDN_V_HEADS = 8
GDN_DK = 128
GDN_DV = 128
GDN_CONV_K = 4
GDN_QK_WIDTH = GDN_QK_HEADS * GDN_DK
GDN_V_WIDTH = GDN_V_HEADS * GDN_DV
GDN_CONV_WIDTH = 2 * GDN_QK_WIDTH + GDN_V_WIDTH
GDN_GROUPS = 2

SRC_A = 0
SRC_B = SRC_A + 4 * CONV_WIDTH
SRC_CQKV = SRC_B + 4 * HGRN_WIDTH
SRC_SMALL = SRC_CQKV + GDN_CONV_WIDTH
SRC_CZ = SRC_SMALL + 2 * GDN_V_HEADS
SRC_G = SRC_CZ + GDN_V_WIDTH
SMALL_COLS = 128

WB_COLS = 4 * HGRN_WIDTH
WC_Q, WC_K, WC_V, WC_Z, WC_SMALL = 0, 512, 1024, 2048, 3072
WC_COLS = WC_SMALL + SMALL_COLS
WM_A, WM_G = 0, 4 * CONV_WIDTH
WM_COLS = WM_G + 3 * D_MODEL

V7X_SUBLANES = 8
V7X_VMEM_BYTES = 64 * 1024 * 1024


def _vmem_limit(block_bytes, resident_bytes, temp_bytes):
    need = 2 * block_bytes + resident_bytes + temp_bytes
    return int(min(V7X_VMEM_BYTES - 8 * 1024 * 1024, need + need // 4 + 4 * 1024 * 1024))


def _resident(shape):
    return pl.BlockSpec(shape, lambda b, s: (0,) * len(shape), pipeline_mode=pl.Buffered(1))


def _layer_rows(width):
    return _resident((DEPTH, width))


def _layer_slab(layer, shape):
    return pl.BlockSpec((None,) + tuple(shape), lambda b, s: (layer,) + (0,) * len(shape),
                        pipeline_mode=pl.Buffered(1))


def _sigmoid(x):
    return 0.5 * jnp.tanh(0.5 * x) + 0.5


def _silu(x):
    return x * _sigmoid(x)


def _softplus(x):
    return jnp.maximum(x, 0.0) + jnp.log(1.0 + jnp.exp(-jnp.abs(x)))


def _dot(a, b):
    return jnp.dot(a.astype(BF16), b.astype(BF16), preferred_element_type=F32)


def _dot_nt(a, b):
    return lax.dot_general(a.astype(BF16), b.astype(BF16), (((1,), (1,)), ((), ())),
                           preferred_element_type=F32)


def _rmsnorm_bf16(x, w):
    ms = jnp.mean(x * x, axis=-1, keepdims=True)
    return (x * lax.rsqrt(ms + NORM_EPS) * w).astype(BF16)


def _proj(h, w_ref, start, width):
    return jnp.dot(h, w_ref[:, start:start + width], preferred_element_type=F32)


def _causal_conv(x, pad_ref, w, width):
    n = x.shape[0]
    pad_ref[V7X_SUBLANES:V7X_SUBLANES + n, :] = x
    acc = x * w[width - 1:width, :]
    for j in range(1, width):
        acc = acc + pad_ref[V7X_SUBLANES - j:V7X_SUBLANES - j + n, :] * w[width - 1 - j:width - j, :]
    pad_ref[0:V7X_SUBLANES, :] = x[n - V7X_SUBLANES:n, :]
    return acc


def _chunk_cumsum_rows(x):
    pos = lax.broadcasted_iota(jnp.int32, x.shape, 0) & (CHUNK - 1)
    d = 1
    while d < CHUNK:
        xr = pltpu.roll(x, d, 0)
        x = x + jnp.where(pos >= d, xr, 0.0)
        d *= 2
    return x


def _chunk_last_rows(x):
    n = x.shape[0]
    parts = [jnp.broadcast_to(x[c * CHUNK + CHUNK - 1:(c + 1) * CHUNK, :], (CHUNK, x.shape[1]))
             for c in range(n // CHUNK)]
    return jnp.concatenate(parts, axis=0)


def _head_rmsnorm(o, w):
    return o * lax.rsqrt(jnp.mean(o * o, axis=-1, keepdims=True) + NORM_EPS) * w


HGRN_LEVELS = (32, 16, 8, 4, 2)


def _hgrn_masks():
    n = CHUNK
    rowi = lax.broadcasted_iota(jnp.int32, (n, n), 0)
    coli = lax.broadcasted_iota(jnp.int32, (n, n), 1)
    levels = []
    for half in HGRN_LEVELS:
        blk = 2 * half
        levels.append(((rowi & -blk) == (coli & -blk)) & ((rowi & (blk - 1)) >= half) & ((coli & (blk - 1)) < half))
    row2 = lax.broadcasted_iota(jnp.int32, (n, 2 * n), 0)
    lane2 = lax.broadcasted_iota(jnp.int32, (n, 2 * n), 1)
    pair = (lane2 == row2) | ((lane2 == row2 + (n - 1)) & ((row2 & 1) == 1))
    sub = lax.broadcasted_iota(jnp.int32, (n, HGRN_DK), 0) & (V7X_SUBLANES - 1)
    return levels, pair, sub


def _level_ref_rows(g, half, sub):
    n = CHUNK
    g3 = g.reshape(n // 8, 8, HGRN_DK)
    bcast8 = lambda r: jnp.broadcast_to(g3[:, r:r + 1, :], g3.shape).reshape(n, HGRN_DK)
    if half >= 8:
        blk = 2 * half
        nb = n // blk
        return jnp.broadcast_to(g.reshape(nb, blk, HGRN_DK)[:, half:half + 1, :],
                                (nb, blk, HGRN_DK)).reshape(n, HGRN_DK)
    if half == 4:
        return bcast8(4)
    return jnp.where(sub < 4, bcast8(2), bcast8(6))


def _hgrn_block(h, w_ref, lb_ref, hw_ref, st_ref, yb_ref, *, layer, tb, between=()):
    n_slots = 4 + len(HGRN_LEVELS) + 1 + tb // CHUNK
    plan = [[] for _ in range(n_slots)]
    for i, work in enumerate(between):
        plan[i * n_slots // len(between)].append(work)

    def run_between():
        for work in plan.pop(0):
            work()

    rows = [lb_ref[i:i + 1, :] for i in range(DEPTH)]
    mx = functools.reduce(jnp.maximum, rows)
    es = [jnp.exp(r - mx) for r in rows]
    tot = functools.reduce(lambda a_, b_: a_ + b_, es)
    lb = jnp.zeros_like(tot)
    for i in range(1, layer + 1):
        lb = lb + es[i] / tot

    f_gate = lb + (1.0 - lb) * _sigmoid(_proj(h, w_ref, HGRN_WIDTH, HGRN_WIDTH))
    f_floor = jnp.maximum(f_gate, MIN_F)
    run_between()
    g_all = _chunk_cumsum_rows(jnp.log(f_floor))
    k_all = 1.0 - f_gate
    kp_all = k_all * pltpu.roll(f_floor, tb - 1, 0)
    run_between()
    q_all = _silu(_proj(h, w_ref, 0, HGRN_WIDTH)) * (HGRN_DK ** -0.5)
    v_all = _proj(h, w_ref, 2 * HGRN_WIDTH, HGRN_WIDTH)
    run_between()
    g_last_all = _chunk_last_rows(g_all)
    qi_all = q_all * jnp.exp(g_all)
    ks_all = k_all * jnp.exp(g_last_all - g_all)
    dec_all = jnp.exp(g_last_all)
    run_between()

    n = CHUNK
    nc = tb // CHUNK
    level_masks, pair_mask, sub = _hgrn_masks()
    tiles = [(slice(c * CHUNK, (c + 1) * CHUNK), slice(hd * HGRN_DK, (hd + 1) * HGRN_DK))
             for hd in range(HGRN_HEADS) for c in range(nc)]
    qs = [q_all[t] for t in tiles]
    ks = [k_all[t] for t in tiles]
    gs = [g_all[t] for t in tiles]
    vs = [v_all[t] for t in tiles]
    scores = [None] * len(tiles)
    for half, mask in zip(HGRN_LEVELS, level_masks):
        es = [jnp.exp(-jnp.abs(g - _level_ref_rows(g, half, sub))) for g in gs]
        ss = [_dot_nt(q * e, k * e) for q, k, e in zip(qs, ks, es)]
        scores = [jnp.where(mask, s, 0.0 if a is None else a) for s, a in zip(ss, scores)]
        run_between()
    s01 = [jnp.where(pair_mask, _dot_nt(q, jnp.concatenate([k, kp_all[t]], axis=0)), 0.0)
           for q, k, t in zip(qs, ks, tiles)]
    scores = [a + (s + pltpu.roll(s, n, 1))[:, :n] for a, s in zip(scores, s01)]
    intra = [_dot(a, v) for a, v in zip(scores, vs)]
    upd = [_dot(v.T, ks_all[t]) for v, t in zip(vs, tiles)]
    run_between()

    zz = _proj(h, w_ref, 3 * HGRN_WIDTH, HGRN_WIDTH)
    hw = hw_ref[layer:layer + 1, :]
    sts = [st_ref[hd] for hd in range(HGRN_HEADS)]
    for c in range(nc):
        for hd in range(HGRN_HEADS):
            i = hd * nc + c
            rs, cs = tiles[i]
            o = intra[i] + _dot_nt(qi_all[rs, cs], sts[hd])
            sts[hd] = dec_all[c * CHUNK:c * CHUNK + 1, cs] * sts[hd] + upd[i]
            yb_ref[rs, cs] = _head_rmsnorm(o, hw) * _silu(zz[rs, cs])
        run_between()
    for hd in range(HGRN_HEADS):
        st_ref[hd] = sts[hd]
    assert not plan


def _l2norm_heads(x, heads, width):
    outs = []
    for hd in range(heads):
        xh = x[:, hd * width:(hd + 1) * width]
        outs.append(xh * lax.rsqrt(jnp.sum(xh * xh, axis=-1, keepdims=True) + L2_EPS))
    return outs


def _unit_lower_inverse_rows(p2, masks):
    upper, blk16, blk32, eye_lower, eye_upper = masks
    n = CHUNK
    zero_eye = eye_upper.astype(BF16)
    cs = [jnp.where(blk16 & ~upper, p, 0.0) + eye_upper for p in p2]
    for _ in range(4):
        cbs = [c.astype(BF16) for c in cs]
        cs = [jnp.dot(cb, jnp.concatenate([cb, zero_eye], axis=0), preferred_element_type=F32) for cb in cbs]
        yield
    for off in (blk32 & ~blk16, ~blk32):
        ys = [_dot(jnp.where(off, p, 0.0)[:, :n], c) for p, c in zip(p2, cs)]
        yield
        cs = [_dot(c + eye_lower, jnp.concatenate([c, y], axis=0)) for c, y in zip(cs, ys)]
        yield
    return cs


def _interleave(*gens):
    live = list(gens)
    while live:
        for g in list(live):
            try:
                next(g)
            except StopIteration:
                live.remove(g)


def _conv_rows(pad_ref, w, rows, width):
    base = V7X_SUBLANES + rows.start
    n = rows.stop - rows.start
    acc = pad_ref[base:base + n, :] * w[width - 1:width, :]
    for j in range(1, width):
        acc = acc + pad_ref[base - j:base - j + n, :] * w[width - 1 - j:width - j, :]
    return acc


def _gdn_kernel(x_ref, nw_ref, w_ref, cw_ref, alog_ref, dtb_ref, hw_ref,
                y_ref, st_ref, cq_ref, ck_ref, cv_ref, *, layer, tb):
    @pl.when(pl.program_id(1) == 0)
    def _():
        st_ref[...] = jnp.zeros_like(st_ref)
        for pad_ref in (cq_ref, ck_ref, cv_ref):
            pad_ref[0:V7X_SUBLANES, :] = jnp.zeros((V7X_SUBLANES, pad_ref.shape[1]), F32)

    n = CHUNK
    nc = tb // CHUNK
    rep = GDN_V_HEADS // GDN_QK_HEADS
    h = _rmsnorm_bf16(x_ref[...], nw_ref[layer:layer + 1, :])
    cw = cw_ref[...]
    cq_ref[V7X_SUBLANES:, :] = _proj(h, w_ref, WC_Q, GDN_QK_WIDTH)
    ck_ref[V7X_SUBLANES:, :] = _proj(h, w_ref, WC_K, GDN_QK_WIDTH)
    cv_ref[V7X_SUBLANES:, :] = _proj(h, w_ref, WC_V, GDN_V_WIDTH)
    sm = _proj(h, w_ref, WC_SMALL, SMALL_COLS)
    zz = _proj(h, w_ref, WC_Z, GDN_V_WIDTH)
    alog = alog_ref[layer:layer + 1, :]
    dtb = dtb_ref[layer:layer + 1, :]
    hw = hw_ref[layer:layer + 1, :]

    row2 = lax.broadcasted_iota(jnp.int32, (n, 2 * n), 0)
    lane2 = lax.broadcasted_iota(jnp.int32, (n, 2 * n), 1)
    col2 = lane2 & (n - 1)
    upper = lane2 >= n
    strict2 = row2 > col2
    causal = lax.broadcasted_iota(jnp.int32, (n, n), 0) >= lax.broadcasted_iota(jnp.int32, (n, n), 1)
    blk16 = (row2 >> 4) == (col2 >> 4)
    blk32 = (row2 >> 5) == (col2 >> 5)
    eye_lower = jnp.where((row2 == col2) & ~upper, 1.0, 0.0).astype(F32)
    eye_upper = jnp.where((row2 == col2) & upper, 1.0, 0.0).astype(F32)
    masks = (upper, blk16, blk32, eye_lower, eye_upper)
    lane = lax.broadcasted_iota(jnp.int32, (n, SMALL_COLS), 1)
    zeros_rhs = jnp.zeros((n, GDN_DV + GDN_DK), F32)

    def local_stage(c):
        rs = slice(c * CHUNK, (c + 1) * CHUNK)
        qc = _silu(_conv_rows(cq_ref, cw[:, 0:GDN_QK_WIDTH], rs, GDN_CONV_K))
        kc = _silu(_conv_rows(ck_ref, cw[:, GDN_QK_WIDTH:2 * GDN_QK_WIDTH], rs, GDN_CONV_K))
        vc = _silu(_conv_rows(cv_ref, cw[:, 2 * GDN_QK_WIDTH:], rs, GDN_CONV_K))
        qs = [x * (GDN_DK ** -0.5) for x in _l2norm_heads(qc, GDN_QK_HEADS, GDN_DK)]
        ks = _l2norm_heads(kc, GDN_QK_HEADS, GDN_DK)
        smc = sm[rs]
        beta = _sigmoid(smc)
        g = _chunk_cumsum_rows(-jnp.exp(alog) * _softplus(smc + dtb))
        comb_t = jnp.where(lane < GDN_V_HEADS, beta, g).T
        g_last = g[n - 1:n, :]
        eg_all = jnp.exp(g)
        kfac_all = pltpu.roll(beta, GDN_V_HEADS, 1) * jnp.exp(g_last - g)
        dec_all = jnp.exp(g_last)
        out = dict(p2=[], aqk=[], rhs=[], qi=[], kst=[], dec=[])
        yield
        for hq in range(GDN_QK_HEADS):
            qh, kh = qs[hq], ks[hq]
            kq2 = _dot_nt(jnp.concatenate([-kh, qh], axis=0), jnp.concatenate([kh, kh], axis=0))
            nkk2 = kq2[:n]
            qk = kq2[n:, :n]
            for r in range(rep):
                vh = hq * rep + r
                gl = GDN_V_HEADS + vh
                g_row = comb_t[gl:gl + 1, :]
                b_row = comb_t[vh:vh + 1, :]
                g_row2 = jnp.concatenate([g_row, g_row], axis=1)
                b_row2 = jnp.concatenate([b_row, b_row], axis=1)
                dec2 = jnp.exp(jnp.minimum(g[:, gl:gl + 1] - g_row2, 0.0)) * b_row2
                eg = eg_all[:, gl:gl + 1]
                out["p2"].append(jnp.where(strict2, nkk2 * dec2, 0.0))
                out["aqk"].append(jnp.where(causal, qk * dec2[:, :n], 0.0))
                out["rhs"].append(jnp.concatenate([vc[:, vh * GDN_DV:(vh + 1) * GDN_DV], kh * eg], axis=1))
                out["qi"].append(qh * eg)
                out["kst"].append(kh * kfac_all[:, gl:gl + 1])
                out["dec"].append(dec_all[:, gl:gl + 1])
            yield
        locs[c] = out

    def solve_stage(loc):
        cs = yield from _unit_lower_inverse_rows(loc["p2"], masks)
        loc["sol"] = [_dot(c_, jnp.concatenate([zeros_rhs, rhs], axis=0)) for c_, rhs in zip(cs, loc["rhs"])]

    def state_stage(c, loc, sts):
        rs = slice(c * CHUNK, (c + 1) * CHUNK)
        wq = [_dot(jnp.concatenate([sol[:, GDN_DV:], qi], axis=0), st) for sol, qi, st in zip(loc["sol"], loc["qi"], sts)]
        es = [sol[:, :GDN_DV] - r[:n] for sol, r in zip(loc["sol"], wq)]
        ae = [_dot(jnp.concatenate([aqk, kst.T], axis=0), e) for aqk, kst, e in zip(loc["aqk"], loc["kst"], es)]
        os_ = [r[n:] + a[:n] for r, a in zip(wq, ae)]
        new = [dec * st + a[n:] for dec, st, a in zip(loc["dec"], sts, ae)]
        for vh in range(GDN_V_HEADS):
            vs = slice(vh * GDN_DV, (vh + 1) * GDN_DV)
            y_ref[rs, vs] = (_head_rmsnorm(os_[vh], hw) * _silu(zz[rs, vs])).astype(BF16)
        return new

    sts = [st_ref[vh] for vh in range(GDN_V_HEADS)]
    locs = {}
    per_group = nc // GDN_GROUPS
    groups = [range(i * per_group, (i + 1) * per_group) for i in range(GDN_GROUPS)]

    def solve_group(chunks):
        merged = {key: [v for c in chunks for v in locs[c][key]] for key in ("p2", "rhs")}
        yield from solve_stage(merged)
        for i, c in enumerate(chunks):
            locs[c]["sol"] = merged["sol"][i * GDN_V_HEADS:(i + 1) * GDN_V_HEADS]

    def local_group(chunks):
        for c in chunks:
            yield from local_stage(c)

    def state_group(chunks):
        nonlocal sts
        for c in chunks:
            sts = state_stage(c, locs.pop(c), sts)
            yield

    for i in range(GDN_GROUPS + 2):
        work = []
        if 0 <= i - 1 < GDN_GROUPS:
            work.append(solve_group(groups[i - 1]))
        if i < GDN_GROUPS:
            work.append(local_group(groups[i]))
        if 0 <= i - 2 < GDN_GROUPS:
            work.append(state_group(groups[i - 2]))
        _interleave(*work)
    for vh in range(GDN_V_HEADS):
        st_ref[vh] = sts[vh]
    for pad_ref in (cq_ref, ck_ref, cv_ref):
        pad_ref[0:V7X_SUBLANES, :] = pad_ref[tb:tb + V7X_SUBLANES, :]


def _gdn(x2, norm_w, w_c, conv_c, alog_pad, dtb_pad, head_w, *, layer, bsz, seq, tb=1024):
    nblk = seq // tb
    row = lambda b, s: (b * nblk + s, 0)
    block_bytes = tb * D_MODEL * 4 + tb * GDN_V_WIDTH * 2
    resident_bytes = (D_MODEL * WC_COLS * 2 + GDN_CONV_K * GDN_CONV_WIDTH * 4
                      + (GDN_V_HEADS * GDN_DK * GDN_DV + (V7X_SUBLANES + tb) * GDN_CONV_WIDTH) * 4)
    temp_bytes = 6 * tb * WC_COLS * 4
    return pl.pallas_call(
        functools.partial(_gdn_kernel, layer=layer, tb=tb),
        grid=(bsz, nblk),
        in_specs=[
            pl.BlockSpec((tb, D_MODEL), row),
            _layer_rows(D_MODEL),
            _layer_slab(layer, (D_MODEL, WC_COLS)),
            _layer_slab(layer, (GDN_CONV_K, GDN_CONV_WIDTH)),
            _layer_rows(SMALL_COLS),
            _layer_rows(SMALL_COLS),
            _layer_rows(GDN_DV),
        ],
        out_specs=pl.BlockSpec((tb, GDN_V_WIDTH), row),
        out_shape=jax.ShapeDtypeStruct((bsz * seq, GDN_V_WIDTH), BF16),
        scratch_shapes=[
            pltpu.VMEM((GDN_V_HEADS, GDN_DK, GDN_DV), F32),
            pltpu.VMEM((V7X_SUBLANES + tb, GDN_QK_WIDTH), F32),
            pltpu.VMEM((V7X_SUBLANES + tb, GDN_QK_WIDTH), F32),
            pltpu.VMEM((V7X_SUBLANES + tb, GDN_V_WIDTH), F32),
        ],
        compiler_params=pltpu.CompilerParams(
            dimension_semantics=("arbitrary", "arbitrary"),
            vmem_limit_bytes=_vmem_limit(block_bytes, resident_bytes, temp_bytes)),
        name="gdn",
    )(x2, norm_w, w_c, conv_c, alog_pad, dtb_pad, head_w)


def _hgrn_merge_kernel(x_ref, nw_ref, wb_ref, lb_ref, hw_ref, wm_ref, yc_ref, ca_ref, bg_ref,
                       wa_ref, wbo_ref, wc_ref, wo_ref, fw_ref,
                       out_ref, st_ref, pad_ref, yb_ref, *, layer, final, tb):
    @pl.when(pl.program_id(1) == 0)
    def _():
        st_ref[...] = jnp.zeros_like(st_ref)
        pad_ref[0:V7X_SUBLANES, :] = jnp.zeros((V7X_SUBLANES, CONV_WIDTH), F32)

    x = x_ref[...]
    h = _rmsnorm_bf16(x, nw_ref[layer:layer + 1, :])
    seg = lambda i: _proj(h, wm_ref, WM_A + i * CONV_WIDTH, CONV_WIDTH)
    bg = bg_ref[layer:layer + 1, :]
    gate = lambda i: _sigmoid(_proj(h, wm_ref, WM_G + i * D_MODEL, D_MODEL) + bg[:, i * D_MODEL:(i + 1) * D_MODEL])
    part = {}

    def conv_a():
        part["conv"] = _causal_conv(seg(1) * seg(2), pad_ref, ca_ref[...], CONV_K)

    def act_a():
        part["ya"] = (seg(0) * part["conv"] * _silu(seg(3))).astype(BF16)

    def branch_a():
        part["ya2"] = jnp.dot(part["ya"], wa_ref[...], preferred_element_type=F32)

    def branch_c():
        part["yc2"] = jnp.dot(yc_ref[...], wc_ref[...], preferred_element_type=F32)

    def gate_a():
        part["merged"] = gate(0) * part["ya2"]

    def gate_c():
        part["merged"] = part["merged"] + gate(2) * part["yc2"]

    def gate_b():
        part["gate_b"] = gate(1)

    _hgrn_block(h, wb_ref, lb_ref, hw_ref, st_ref, yb_ref, layer=layer, tb=tb,
                between=(conv_a, act_a, branch_a, branch_c, gate_a, gate_c, gate_b))

    yb2 = jnp.dot(yb_ref[...].astype(BF16), wbo_ref[...], preferred_element_type=F32)
    merged = part["merged"] + part["gate_b"] * yb2
    out = x + jnp.dot(merged.astype(BF16), wo_ref[...], preferred_element_type=F32)
    if final:
        ms = jnp.mean(out * out, axis=-1, keepdims=True)
        out = out * lax.rsqrt(ms + NORM_EPS) * fw_ref[...]
    out_ref[...] = out


def _hgrn_merge(x2, norm_w, w_b, lower_bounds, head_w, w_m, yc, conv_a, b_gate, wa, wb, wc, wo, final_w,
                *, layer, final, bsz, seq, tb=512):
    nblk = seq // tb
    row = lambda b, s: (b * nblk + s, 0)
    block_bytes = tb * (2 * D_MODEL * 4 + GDN_V_WIDTH * 2)
    resident_bytes = ((D_MODEL * (WB_COLS + WM_COLS) + (CONV_WIDTH + HGRN_WIDTH + GDN_V_WIDTH + D_MODEL) * D_MODEL) * 2
                      + (HGRN_HEADS * HGRN_DK * HGRN_DK + (V7X_SUBLANES + tb) * CONV_WIDTH + tb * HGRN_WIDTH) * 4)
    temp_bytes = 4 * tb * (WB_COLS + WM_COLS) * 4
    return pl.pallas_call(
        functools.partial(_hgrn_merge_kernel, layer=layer, final=final, tb=tb),
        grid=(bsz, nblk),
        in_specs=[
            pl.BlockSpec((tb, D_MODEL), row),
            _layer_rows(D_MODEL),
            _layer_slab(layer, (D_MODEL, WB_COLS)),
            _resident((DEPTH, HGRN_WIDTH)),
            _layer_rows(HGRN_DK),
            _layer_slab(layer, (D_MODEL, WM_COLS)),
            pl.BlockSpec((tb, GDN_V_WIDTH), row),
            _layer_slab(layer, (CONV_K, CONV_WIDTH)),
            _layer_rows(3 * D_MODEL),
            _layer_slab(layer, (CONV_WIDTH, D_MODEL)),
            _layer_slab(layer, (HGRN_WIDTH, D_MODEL)),
            _layer_slab(layer, (GDN_V_WIDTH, D_MODEL)),
            _layer_slab(layer, (D_MODEL, D_MODEL)),
            _resident((1, D_MODEL)),
        ],
        out_specs=pl.BlockSpec((tb, D_MODEL), row),
        out_shape=jax.ShapeDtypeStruct((bsz * seq, D_MODEL), F32),
        scratch_shapes=[
            pltpu.VMEM((HGRN_HEADS, HGRN_DK, HGRN_DK), F32),
            pltpu.VMEM((V7X_SUBLANES + tb, CONV_WIDTH), F32),
            pltpu.VMEM((tb, HGRN_WIDTH), F32),
        ],
        compiler_params=pltpu.CompilerParams(
            dimension_semantics=("arbitrary", "arbitrary"),
            vmem_limit_bytes=_vmem_limit(block_bytes, resident_bytes, temp_bytes)),
        name="hgrn_merge_final" if final else "hgrn_merge",
    )(x2, norm_w, w_b, lower_bounds, head_w, w_m, yc, conv_a, b_gate, wa, wb, wc, wo, final_w)


PREP_ROWS = 128
PREP_COLS = 512
PREP_NB, PREP_NC, PREP_NM = WB_COLS // PREP_COLS, -(-WC_COLS // PREP_COLS), WM_COLS // PREP_COLS


def _prep_sources():
    span = lambda start, width: [start + i * PREP_COLS for i in range(width // PREP_COLS)]
    return (span(SRC_B, WB_COLS)
            + span(SRC_CQKV, GDN_CONV_WIDTH) + span(SRC_CZ, GDN_V_WIDTH) + [SRC_SMALL]
            + span(SRC_A, 4 * CONV_WIDTH) + span(SRC_G, 3 * D_MODEL))


def _prep_in_kernel(src_ref, wt_ref, pb_ref, pc_ref, pm_ref):
    del src_ref
    j = pl.program_id(1)
    lane = lax.broadcasted_iota(jnp.int32, (D_MODEL, PREP_COLS), 1)
    pad = jnp.logical_and(j == PREP_NB + PREP_NC - 1, lane >= 2 * GDN_V_HEADS)
    blk = jnp.where(pad, 0.0, wt_ref[0].T).astype(BF16)

    @pl.when(j < PREP_NB)
    def _():
        pb_ref[...] = blk

    @pl.when(jnp.logical_and(j >= PREP_NB, j < PREP_NB + PREP_NC))
    def _():
        pc_ref[...] = blk

    @pl.when(j >= PREP_NB + PREP_NC)
    def _():
        pm_ref[...] = blk


def _prep_in_weights(w_in):
    w_t = jnp.swapaxes(w_in, 1, 2)
    src = jnp.asarray([c // V7X_SUBLANES for c in _prep_sources()], jnp.int32)
    out_spec = lambda first, count: pl.BlockSpec(
        (None, D_MODEL, PREP_COLS), lambda l, j, src_ref: (l, 0, jnp.clip(j - first, 0, count - 1)))
    out = lambda cols: jax.ShapeDtypeStruct((DEPTH, D_MODEL, cols), BF16)
    block_bytes = PREP_COLS * D_MODEL * (4 + 3 * 2)
    return pl.pallas_call(
        _prep_in_kernel,
        grid_spec=pltpu.PrefetchScalarGridSpec(
            num_scalar_prefetch=1,
            grid=(DEPTH, PREP_NB + PREP_NC + PREP_NM),
            in_specs=[pl.BlockSpec((pl.Element(1), pl.Element(PREP_COLS), pl.Element(D_MODEL)),
                                   lambda l, j, src_ref: (l, src_ref[j] * V7X_SUBLANES, 0))],
            out_specs=[out_spec(0, PREP_NB), out_spec(PREP_NB, PREP_NC), out_spec(PREP_NB + PREP_NC, PREP_NM)],
        ),
        out_shape=[out(WB_COLS), out(WC_COLS), out(WM_COLS)],
        compiler_params=pltpu.CompilerParams(
            dimension_semantics=("arbitrary", "arbitrary"),
            vmem_limit_bytes=_vmem_limit(block_bytes, 0, 4 * PREP_COLS * D_MODEL * 4)),
        name="prep_in_weights",
    )(src, w_t)


def _prep_out_kernel(wa_ref, wb_ref, wc_ref, wo_ref, oa_ref, ob_ref, oc_ref, oo_ref):
    oa_ref[...] = wa_ref[...].astype(BF16)
    ob_ref[...] = wb_ref[...].astype(BF16)
    oc_ref[...] = wc_ref[...].astype(BF16)
    oo_ref[...] = wo_ref[...].astype(BF16)


def _prep_out_weights(w_out_a, w_out_b, w_out_c, w_o):
    steps = D_MODEL // PREP_ROWS
    rows = (CONV_WIDTH, HGRN_WIDTH, GDN_V_WIDTH, D_MODEL)
    slab = lambda r: pl.BlockSpec((None, r // steps, D_MODEL), lambda l, i: (l, i, 0))
    return pl.pallas_call(
        _prep_out_kernel,
        grid=(DEPTH, steps),
        in_specs=[slab(r) for r in rows],
        out_specs=[slab(r) for r in rows],
        out_shape=[jax.ShapeDtypeStruct((DEPTH, r, D_MODEL), BF16) for r in rows],
        compiler_params=pltpu.CompilerParams(
            dimension_semantics=("arbitrary", "arbitrary"),
            vmem_limit_bytes=_vmem_limit(sum(rows) // steps * D_MODEL * 6, 0, 0)),
        name="prep_out_weights",
    )(w_out_a, w_out_b, w_out_c, w_o)


def kernel(x, norm_w, w_in, b_gate, conv_a, conv_c, a_log, dt_bias, lower_bounds, hgrn_norm_w, gdn_norm_w,
           w_out_a, w_out_b, w_out_c, w_o, final_norm_w):
    bsz, seq, _ = x.shape
    x2 = x.reshape(bsz * seq, D_MODEL)
    w_b, w_c, w_m = _prep_in_weights(w_in)
    wa, wb, wc, wo = _prep_out_weights(w_out_a, w_out_b, w_out_c, w_o)
    pad8 = lambda a: jnp.pad(a.astype(F32), ((0, 0), (GDN_V_HEADS, SMALL_COLS - 2 * GDN_V_HEADS)))
    alog_pad, dtb_pad = pad8(a_log), pad8(dt_bias)
    final_w = final_norm_w[None, :]
    for l in range(DEPTH):
        yc = _gdn(x2, norm_w, w_c, conv_c, alog_pad, dtb_pad, gdn_norm_w, layer=l, bsz=bsz, seq=seq)
        x2 = _hgrn_merge(x2, norm_w, w_b, lower_bounds, hgrn_norm_w, w_m, yc, conv_a, b_gate, wa, wb, wc, wo,
                         final_w, layer=l, final=(l == DEPTH - 1), bsz=bsz, seq=seq)
    return x2.reshape(bsz, seq, D_MODEL)
```

```python
import functools

import jax
import jax.numpy as jnp
from jax import lax
from jax.experimental import pallas as pl
from jax.experimental.pallas import tpu as pltpu

F32 = jnp.float32
BF16 = jnp.bfloat16

D_MODEL = 1024
DEPTH = 2
CHUNK = 64
NORM_EPS = 1e-6
L2_EPS = 1e-6
MIN_F = 1e-30
CONV_WIDTH = 512
CONV_K = 3
HGRN_HEADS = 4
HGRN_DK = 128
HGRN_WIDTH = HGRN_HEADS * HGRN_DK
GDN_QK_HEADS = 4
GDN_V_HEADS = 8
GDN_DK = 128
GDN_DV = 128
GDN_CONV_K = 4
GDN_QK_WIDTH = GDN_QK_HEADS * GDN_DK
GDN_V_WIDTH = GDN_V_HEADS * GDN_DV
GDN_CONV_WIDTH = 2 * GDN_QK_WIDTH + GDN_V_WIDTH
GDN_GROUPS = 2

SRC_A = 0
SRC_B = SRC_A + 4 * CONV_WIDTH
SRC_CQKV = SRC_B + 4 * HGRN_WIDTH
SRC_SMALL = SRC_CQKV + GDN_CONV_WIDTH
SRC_CZ = SRC_SMALL + 2 * GDN_V_HEADS
SRC_G = SRC_CZ + GDN_V_WIDTH
SMALL_COLS = 128

WB_COLS = 4 * HGRN_WIDTH
WC_Q, WC_K, WC_V, WC_Z, WC_SMALL = 0, 512, 1024, 2048, 3072
WC_COLS = WC_SMALL + SMALL_COLS
WM_A, WM_G = 0, 4 * CONV_WIDTH
WM_COLS = WM_G + 3 * D_MODEL

V7X_SUBLANES = 8
V7X_VMEM_BYTES = 64 * 1024 * 1024


def _vmem_limit(block_bytes, resident_bytes, temp_bytes):
    need = 2 * block_bytes + resident_bytes + temp_bytes
    return int(min(V7X_VMEM_BYTES - 8 * 1024 * 1024, need + need // 4 + 4 * 1024 * 1024))


def _resident(shape):
    return pl.BlockSpec(shape, lambda b, s: (0,) * len(shape), pipeline_mode=pl.Buffered(1))


def _layer_rows(width):
    return _resident((DEPTH, width))


def _layer_slab(layer, shape):
    return pl.BlockSpec((None,) + tuple(shape), lambda b, s: (layer,) + (0,) * len(shape),
                        pipeline_mode=pl.Buffered(1))


def _sigmoid(x):
    return 0.5 * jnp.tanh(0.5 * x) + 0.5


def _silu(x):
    return x * _sigmoid(x)


def _softplus(x):
    return jnp.maximum(x, 0.0) + jnp.log(1.0 + jnp.exp(-jnp.abs(x)))


def _dot(a, b):
    return jnp.dot(a.astype(BF16), b.astype(BF16), preferred_element_type=F32)


def _dot_nt(a, b):
    return lax.dot_general(a.astype(BF16), b.astype(BF16), (((1,), (1,)), ((), ())),
                           preferred_element_type=F32)


def _rmsnorm_bf16(x, w):
    ms = jnp.mean(x * x, axis=-1, keepdims=True)
    return (x * lax.rsqrt(ms + NORM_EPS) * w).astype(BF16)


def _proj(h, w_ref, start, width):
    return jnp.dot(h, w_ref[:, start:start + width], preferred_element_type=F32)


def _causal_conv(x, pad_ref, w, width):
    n = x.shape[0]
    pad_ref[V7X_SUBLANES:V7X_SUBLANES + n, :] = x
    acc = x * w[width - 1:width, :]
    for j in range(1, width):
        acc = acc + pad_ref[V7X_SUBLANES - j:V7X_SUBLANES - j + n, :] * w[width - 1 - j:width - j, :]
    pad_ref[0:V7X_SUBLANES, :] = x[n - V7X_SUBLANES:n, :]
    return acc


def _chunk_cumsum_rows(x):
    pos = lax.broadcasted_iota(jnp.int32, x.shape, 0) & (CHUNK - 1)
    d = 1
    while d < CHUNK:
        xr = pltpu.roll(x, d, 0)
        x = x + jnp.where(pos >= d, xr, 0.0)
        d *= 2
    return x


def _chunk_last_rows(x):
    n = x.shape[0]
    parts = [jnp.broadcast_to(x[c * CHUNK + CHUNK - 1:(c + 1) * CHUNK, :], (CHUNK, x.shape[1]))
             for c in range(n // CHUNK)]
    return jnp.concatenate(parts, axis=0)


def _head_rmsnorm(o, w):
    return o * lax.rsqrt(jnp.mean(o * o, axis=-1, keepdims=True) + NORM_EPS) * w


HGRN_LEVELS = (32, 16, 8, 4, 2)
HGRN_GROUPS = 2


def _hgrn_masks():
    n = CHUNK
    rowi = lax.broadcasted_iota(jnp.int32, (n, n), 0)
    coli = lax.broadcasted_iota(jnp.int32, (n, n), 1)
    levels = []
    for half in HGRN_LEVELS:
        blk = 2 * half
        levels.append(((rowi & -blk) == (coli & -blk)) & ((rowi & (blk - 1)) >= half) & ((coli & (blk - 1)) < half))
    row2 = lax.broadcasted_iota(jnp.int32, (n, 2 * n), 0)
    lane2 = lax.broadcasted_iota(jnp.int32, (n, 2 * n), 1)
    pair = (lane2 == row2) | ((lane2 == row2 + (n - 1)) & ((row2 & 1) == 1))
    sub = lax.broadcasted_iota(jnp.int32, (n, HGRN_DK), 0) & (V7X_SUBLANES - 1)
    return levels, pair, sub


def _level_ref_rows(g, half, sub):
    n = CHUNK
    g3 = g.reshape(n // 8, 8, HGRN_DK)
    bcast8 = lambda r: jnp.broadcast_to(g3[:, r:r + 1, :], g3.shape).reshape(n, HGRN_DK)
    if half >= 8:
        blk = 2 * half
        nb = n // blk
        return jnp.broadcast_to(g.reshape(nb, blk, HGRN_DK)[:, half:half + 1, :],
                                (nb, blk, HGRN_DK)).reshape(n, HGRN_DK)
    if half == 4:
        return bcast8(4)
    return jnp.where(sub < 4, bcast8(2), bcast8(6))


def _hgrn_block(h, w_ref, lb_ref, hw_ref, st_ref, yb_ref, *, layer, tb, between=()):
    n_slots = 4 + HGRN_GROUPS * (len(HGRN_LEVELS) + 1) + tb // CHUNK
    plan = [[] for _ in range(n_slots)]
    for i, work in enumerate(between):
        plan[i * n_slots // len(between)].append(work)

    def run_between():
        for work in plan.pop(0):
            work()

    rows = [lb_ref[i:i + 1, :] for i in range(DEPTH)]
    mx = functools.reduce(jnp.maximum, rows)
    es = [jnp.exp(r - mx) for r in rows]
    tot = functools.reduce(lambda a_, b_: a_ + b_, es)
    lb = jnp.zeros_like(tot)
    for i in range(1, layer + 1):
        lb = lb + es[i] / tot

    f_gate = lb + (1.0 - lb) * _sigmoid(_proj(h, w_ref, HGRN_WIDTH, HGRN_WIDTH))
    f_floor = jnp.maximum(f_gate, MIN_F)
    run_between()
    g_all = _chunk_cumsum_rows(jnp.log(f_floor))
    k_all = 1.0 - f_gate
    kp_all = k_all * pltpu.roll(f_floor, tb - 1, 0)
    run_between()
    q_all = _silu(_proj(h, w_ref, 0, HGRN_WIDTH)) * (HGRN_DK ** -0.5)
    v_all = _proj(h, w_ref, 2 * HGRN_WIDTH, HGRN_WIDTH)
    run_between()
    g_last_all = _chunk_last_rows(g_all)
    qi_all = q_all * jnp.exp(g_all)
    ks_all = k_all * jnp.exp(g_last_all - g_all)
    dec_all = jnp.exp(g_last_all)
    run_between()

    n = CHUNK
    nc = tb // CHUNK
    level_masks, pair_mask, sub = _hgrn_masks()
    tiles = [(slice(c * CHUNK, (c + 1) * CHUNK), slice(hd * HGRN_DK, (hd + 1) * HGRN_DK))
             for hd in range(HGRN_HEADS) for c in range(nc)]
    intra, upd = {}, {}
    per_group = len(tiles) // HGRN_GROUPS
    for grp in range(HGRN_GROUPS):
        ids = range(grp * per_group, (grp + 1) * per_group)
        qs = [q_all[tiles[i]] for i in ids]
        ks = [k_all[tiles[i]] for i in ids]
        gs = [g_all[tiles[i]] for i in ids]
        vs = [v_all[tiles[i]] for i in ids]
        scores = [None] * per_group
        for half, mask in zip(HGRN_LEVELS, level_masks):
            es = [jnp.exp(-jnp.abs(g - _level_ref_rows(g, half, sub))) for g in gs]
            ss = [_dot_nt(q * e, k * e) for q, k, e in zip(qs, ks, es)]
            scores = [jnp.where(mask, s, 0.0 if a is None else a) for s, a in zip(ss, scores)]
            run_between()
        s01 = [jnp.where(pair_mask, _dot_nt(q, jnp.concatenate([k, kp_all[tiles[i]]], axis=0)), 0.0)
               for q, k, i in zip(qs, ks, ids)]
        scores = [a + (s + pltpu.roll(s, n, 1))[:, :n] for a, s in zip(scores, s01)]
        for i, a, v in zip(ids, scores, vs):
            intra[i] = _dot(a, v)
            upd[i] = _dot(v.T, ks_all[tiles[i]])
        run_between()

    zz = _proj(h, w_ref, 3 * HGRN_WIDTH, HGRN_WIDTH)
    hw = hw_ref[layer:layer + 1, :]
    sts = [st_ref[hd] for hd in range(HGRN_HEADS)]
    for c in range(nc):
        for hd in range(HGRN_HEADS):
            i = hd * nc + c
            rs, cs = tiles[i]
            o = intra[i] + _dot_nt(qi_all[rs, cs], sts[hd])
            sts[hd] = dec_all[c * CHUNK:c * CHUNK + 1, cs] * sts[hd] + upd[i]
            yb_ref[rs, cs] = _head_rmsnorm(o, hw) * _silu(zz[rs, cs])
        run_between()
    for hd in range(HGRN_HEADS):
        st_ref[hd] = sts[hd]
    assert not plan


def _l2norm_heads(x, heads, width):
    outs = []
    for hd in range(heads):
        xh = x[:, hd * width:(hd + 1) * width]
        outs.append(xh * lax.rsqrt(jnp.sum(xh * xh, axis=-1, keepdims=True) + L2_EPS))
    return outs


def _unit_lower_inverse_rows(p2, masks):
    upper, blk16, blk32, eye_lower, eye_upper = masks
    n = CHUNK
    zero_eye = eye_upper.astype(BF16)
    cs = [jnp.where(blk16 & ~upper, p, 0.0) + eye_upper for p in p2]
    for _ in range(4):
        cbs = [c.astype(BF16) for c in cs]
        rs = [jnp.dot(cb[:, :n], cb, preferred_element_type=F32) for cb in cbs]
        cs = [r + jnp.where(upper, c, 0.0) for r, c in zip(rs, cs)]
        yield
    for off in (blk32 & ~blk16, ~blk32):
        ys = [_dot(jnp.where(off, p, 0.0)[:, :n], c) for p, c in zip(p2, cs)]
        yield
        zeros = jnp.zeros((n, 2 * n), F32)
        cs = [c + _dot(c, jnp.concatenate([zeros, y], axis=0)) for c, y in zip(cs, ys)]
        yield
    return cs


def _interleave(*gens):
    live = list(gens)
    while live:
        for g in list(live):
            try:
                next(g)
            except StopIteration:
                live.remove(g)


def _conv_rows(pad_ref, w, rows, width):
    base = V7X_SUBLANES + rows.start
    n = rows.stop - rows.start
    acc = pad_ref[base:base + n, :] * w[width - 1:width, :]
    for j in range(1, width):
        acc = acc + pad_ref[base - j:base - j + n, :] * w[width - 1 - j:width - j, :]
    return acc


def _gdn_kernel(x_ref, nw_ref, w_ref, cw_ref, alog_ref, dtb_ref, hw_ref,
                y_ref, st_ref, cq_ref, ck_ref, cv_ref, *, layer, tb):
    @pl.when(pl.program_id(1) == 0)
    def _():
        st_ref[...] = jnp.zeros_like(st_ref)
        for pad_ref in (cq_ref, ck_ref, cv_ref):
            pad_ref[0:V7X_SUBLANES, :] = jnp.zeros((V7X_SUBLANES, pad_ref.shape[1]), F32)

    n = CHUNK
    nc = tb // CHUNK
    rep = GDN_V_HEADS // GDN_QK_HEADS
    h = _rmsnorm_bf16(x_ref[...], nw_ref[layer:layer + 1, :])
    cw = cw_ref[...]
    cq_ref[V7X_SUBLANES:, :] = _proj(h, w_ref, WC_Q, GDN_QK_WIDTH)
    ck_ref[V7X_SUBLANES:, :] = _proj(h, w_ref, WC_K, GDN_QK_WIDTH)
    cv_ref[V7X_SUBLANES:, :] = _proj(h, w_ref, WC_V, GDN_V_WIDTH)
    sm = _proj(h, w_ref, WC_SMALL, SMALL_COLS)
    zz = _proj(h, w_ref, WC_Z, GDN_V_WIDTH)
    alog = alog_ref[layer:layer + 1, :]
    dtb = dtb_ref[layer:layer + 1, :]
    hw = hw_ref[layer:layer + 1, :]

    row2 = lax.broadcasted_iota(jnp.int32, (n, 2 * n), 0)
    lane2 = lax.broadcasted_iota(jnp.int32, (n, 2 * n), 1)
    col2 = lane2 & (n - 1)
    upper = lane2 >= n
    strict2 = row2 > col2
    causal = lax.broadcasted_iota(jnp.int32, (n, n), 0) >= lax.broadcasted_iota(jnp.int32, (n, n), 1)
    blk16 = (row2 >> 4) == (col2 >> 4)
    blk32 = (row2 >> 5) == (col2 >> 5)
    eye_lower = jnp.where((row2 == col2) & ~upper, 1.0, 0.0).astype(F32)
    eye_upper = jnp.where((row2 == col2) & upper, 1.0, 0.0).astype(F32)
    masks = (upper, blk16, blk32, eye_lower, eye_upper)
    lane = lax.broadcasted_iota(jnp.int32, (n, SMALL_COLS), 1)
    zeros_rhs = jnp.zeros((n, GDN_DV + GDN_DK), F32)

    def local_stage(c):
        rs = slice(c * CHUNK, (c + 1) * CHUNK)
        qc = _silu(_conv_rows(cq_ref, cw[:, 0:GDN_QK_WIDTH], rs, GDN_CONV_K))
        kc = _silu(_conv_rows(ck_ref, cw[:, GDN_QK_WIDTH:2 * GDN_QK_WIDTH], rs, GDN_CONV_K))
        vc = _silu(_conv_rows(cv_ref, cw[:, 2 * GDN_QK_WIDTH:], rs, GDN_CONV_K))
        qs = [x * (GDN_DK ** -0.5) for x in _l2norm_heads(qc, GDN_QK_HEADS, GDN_DK)]
        ks = _l2norm_heads(kc, GDN_QK_HEADS, GDN_DK)
        smc = sm[rs]
        beta = _sigmoid(smc)
        g = _chunk_cumsum_rows(-jnp.exp(alog) * _softplus(smc + dtb))
        comb_t = jnp.where(lane < GDN_V_HEADS, beta, g).T
        g_last = g[n - 1:n, :]
        eg_all = jnp.exp(g)
        kfac_all = pltpu.roll(beta, GDN_V_HEADS, 1) * jnp.exp(g_last - g)
        dec_all = jnp.exp(g_last)
        out = dict(p2=[], aqk=[], rhs=[], qi=[], kst=[], dec=[])
        yield
        for hq in range(GDN_QK_HEADS):
            qh, kh = qs[hq], ks[hq]
            kq2 = _dot_nt(jnp.concatenate([-kh, qh], axis=0), jnp.concatenate([kh, kh], axis=0))
            nkk2 = kq2[:n]
            qk = kq2[n:, :n]
            for r in range(rep):
                vh = hq * rep + r
                gl = GDN_V_HEADS + vh
                g_row = comb_t[gl:gl + 1, :]
                b_row = comb_t[vh:vh + 1, :]
                g_row2 = jnp.concatenate([g_row, g_row], axis=1)
                b_row2 = jnp.concatenate([b_row, b_row], axis=1)
                dec2 = jnp.exp(jnp.minimum(g[:, gl:gl + 1] - g_row2, 0.0)) * b_row2
                eg = eg_all[:, gl:gl + 1]
                out["p2"].append(jnp.where(strict2, nkk2 * dec2, 0.0))
                out["aqk"].append(jnp.where(causal, qk * dec2[:, :n], 0.0))
                out["rhs"].append(jnp.concatenate([vc[:, vh * GDN_DV:(vh + 1) * GDN_DV], kh * eg], axis=1))
                out["qi"].append(qh * eg)
                out["kst"].append(kh * kfac_all[:, gl:gl + 1])
                out["dec"].append(dec_all[:, gl:gl + 1])
            yield
        locs[c] = out

    def solve_stage(loc):
        cs = yield from _unit_lower_inverse_rows(loc["p2"], masks)
        loc["sol"] = [_dot(c_, jnp.concatenate([zeros_rhs, rhs], axis=0)) for c_, rhs in zip(cs, loc["rhs"])]

    def state_stage(c, loc, sts):
        rs = slice(c * CHUNK, (c + 1) * CHUNK)
        wq = [_dot(jnp.concatenate([sol[:, GDN_DV:], qi], axis=0), st) for sol, qi, st in zip(loc["sol"], loc["qi"], sts)]
        es = [sol[:, :GDN_DV] - r[:n] for sol, r in zip(loc["sol"], wq)]
        ae = [_dot(jnp.concatenate([aqk, kst.T], axis=0), e) for aqk, kst, e in zip(loc["aqk"], loc["kst"], es)]
        os_ = [r[n:] + a[:n] for r, a in zip(wq, ae)]
        new = [dec * st + a[n:] for dec, st, a in zip(loc["dec"], sts, ae)]
        for vh in range(GDN_V_HEADS):
            vs = slice(vh * GDN_DV, (vh + 1) * GDN_DV)
            y_ref[rs, vs] = (_head_rmsnorm(os_[vh], hw) * _silu(zz[rs, vs])).astype(BF16)
        return new

    sts = [st_ref[vh] for vh in range(GDN_V_HEADS)]
    locs = {}
    per_group = nc // GDN_GROUPS
    groups = [range(i * per_group, (i + 1) * per_group) for i in range(GDN_GROUPS)]

    def solve_group(chunks):
        merged = {key: [v for c in chunks for v in locs[c][key]] for key in ("p2", "rhs")}
        yield from solve_stage(merged)
        for i, c in enumerate(chunks):
            locs[c]["sol"] = merged["sol"][i * GDN_V_HEADS:(i + 1) * GDN_V_HEADS]

    def local_group(chunks):
        for c in chunks:
            yield from local_stage(c)

    def state_group(chunks):
        nonlocal sts
        for c in chunks:
            sts = state_stage(c, locs.pop(c), sts)
            yield

    for i in range(GDN_GROUPS + 2):
        work = []
        if 0 <= i - 1 < GDN_GROUPS:
            work.append(solve_group(groups[i - 1]))
        if i < GDN_GROUPS:
            work.append(local_group(groups[i]))
        if 0 <= i - 2 < GDN_GROUPS:
            work.append(state_group(groups[i - 2]))
        _interleave(*work)
    for vh in range(GDN_V_HEADS):
        st_ref[vh] = sts[vh]
    for pad_ref in (cq_ref, ck_ref, cv_ref):
        pad_ref[0:V7X_SUBLANES, :] = pad_ref[tb:tb + V7X_SUBLANES, :]


def _gdn(x2, norm_w, w_c, conv_c, alog_pad, dtb_pad, head_w, *, layer, bsz, seq, tb=512):
    nblk = seq // tb
    row = lambda b, s: (b * nblk + s, 0)
    block_bytes = tb * D_MODEL * 4 + tb * GDN_V_WIDTH * 2
    resident_bytes = (D_MODEL * WC_COLS * 2 + GDN_CONV_K * GDN_CONV_WIDTH * 4
                      + (GDN_V_HEADS * GDN_DK * GDN_DV + (V7X_SUBLANES + tb) * GDN_CONV_WIDTH) * 4)
    temp_bytes = 6 * tb * WC_COLS * 4
    return pl.pallas_call(
        functools.partial(_gdn_kernel, layer=layer, tb=tb),
        grid=(bsz, nblk),
        in_specs=[
            pl.BlockSpec((tb, D_MODEL), row),
            _layer_rows(D_MODEL),
            _layer_slab(layer, (D_MODEL, WC_COLS)),
            _layer_slab(layer, (GDN_CONV_K, GDN_CONV_WIDTH)),
            _layer_rows(SMALL_COLS),
            _layer_rows(SMALL_COLS),
            _layer_rows(GDN_DV),
        ],
        out_specs=pl.BlockSpec((tb, GDN_V_WIDTH), row),
        out_shape=jax.ShapeDtypeStruct((bsz * seq, GDN_V_WIDTH), BF16),
        scratch_shapes=[
            pltpu.VMEM((GDN_V_HEADS, GDN_DK, GDN_DV), F32),
            pltpu.VMEM((V7X_SUBLANES + tb, GDN_QK_WIDTH), F32),
            pltpu.VMEM((V7X_SUBLANES + tb, GDN_QK_WIDTH), F32),
            pltpu.VMEM((V7X_SUBLANES + tb, GDN_V_WIDTH), F32),
        ],
        compiler_params=pltpu.CompilerParams(
            dimension_semantics=("arbitrary", "arbitrary"),
            vmem_limit_bytes=_vmem_limit(block_bytes, resident_bytes, temp_bytes)),
        name="gdn",
    )(x2, norm_w, w_c, conv_c, alog_pad, dtb_pad, head_w)


def _hgrn_merge_kernel(x_ref, nw_ref, wb_ref, lb_ref, hw_ref, wm_ref, yc_ref, ca_ref, bg_ref,
                       wa_ref, wbo_ref, wc_ref, wo_ref, fw_ref,
                       out_ref, st_ref, pad_ref, yb_ref, *, layer, final, tb):
    @pl.when(pl.program_id(1) == 0)
    def _():
        st_ref[...] = jnp.zeros_like(st_ref)
        pad_ref[0:V7X_SUBLANES, :] = jnp.zeros((V7X_SUBLANES, CONV_WIDTH), F32)

    x = x_ref[...]
    h = _rmsnorm_bf16(x, nw_ref[layer:layer + 1, :])
    seg = lambda i: _proj(h, wm_ref, WM_A + i * CONV_WIDTH, CONV_WIDTH)
    bg = bg_ref[layer:layer + 1, :]
    gate = lambda i: _sigmoid(_proj(h, wm_ref, WM_G + i * D_MODEL, D_MODEL) + bg[:, i * D_MODEL:(i + 1) * D_MODEL])
    part = {}

    def conv_a():
        part["conv"] = _causal_conv(seg(1) * seg(2), pad_ref, ca_ref[...], CONV_K)

    def act_a():
        part["ya"] = (seg(0) * part["conv"] * _silu(seg(3))).astype(BF16)

    def branch_a():
        part["ya2"] = jnp.dot(part["ya"], wa_ref[...], preferred_element_type=F32)

    def branch_c():
        part["yc2"] = jnp.dot(yc_ref[...], wc_ref[...], preferred_element_type=F32)

    def gate_a():
        part["merged"] = gate(0) * part["ya2"]

    def gate_c():
        part["merged"] = part["merged"] + gate(2) * part["yc2"]

    def gate_b():
        part["gate_b"] = gate(1)

    _hgrn_block(h, wb_ref, lb_ref, hw_ref, st_ref, yb_ref, layer=layer, tb=tb,
                between=(conv_a, act_a, branch_a, branch_c, gate_a, gate_c, gate_b))

    yb2 = jnp.dot(yb_ref[...].astype(BF16), wbo_ref[...], preferred_element_type=F32)
    merged = part["merged"] + part["gate_b"] * yb2
    out = x + jnp.dot(merged.astype(BF16), wo_ref[...], preferred_element_type=F32)
    if final:
        ms = jnp.mean(out * out, axis=-1, keepdims=True)
        out = out * lax.rsqrt(ms + NORM_EPS) * fw_ref[...]
    out_ref[...] = out


def _hgrn_merge(x2, norm_w, w_b, lower_bounds, head_w, w_m, yc, conv_a, b_gate, wa, wb, wc, wo, final_w,
                *, layer, final, bsz, seq, tb=512):
    nblk = seq // tb
    row = lambda b, s: (b * nblk + s, 0)
    block_bytes = tb * (2 * D_MODEL * 4 + GDN_V_WIDTH * 2)
    resident_bytes = ((D_MODEL * (WB_COLS + WM_COLS) + (CONV_WIDTH + HGRN_WIDTH + GDN_V_WIDTH + D_MODEL) * D_MODEL) * 2
                      + (HGRN_HEADS * HGRN_DK * HGRN_DK + (V7X_SUBLANES + tb) * CONV_WIDTH + tb * HGRN_WIDTH) * 4)
    temp_bytes = 4 * tb * (WB_COLS + WM_COLS) * 4
    return pl.pallas_call(
        functools.partial(_hgrn_merge_kernel, layer=layer, final=final, tb=tb),
        grid=(bsz, nblk),
        in_specs=[
            pl.BlockSpec((tb, D_MODEL), row),
            _layer_rows(D_MODEL),
            _layer_slab(layer, (D_MODEL, WB_COLS)),
            _resident((DEPTH, HGRN_WIDTH)),
            _layer_rows(HGRN_DK),
            _layer_slab(layer, (D_MODEL, WM_COLS)),
            pl.BlockSpec((tb, GDN_V_WIDTH), row),
            _layer_slab(layer, (CONV_K, CONV_WIDTH)),
            _layer_rows(3 * D_MODEL),
            _layer_slab(layer, (CONV_WIDTH, D_MODEL)),
            _layer_slab(layer, (HGRN_WIDTH, D_MODEL)),
            _layer_slab(layer, (GDN_V_WIDTH, D_MODEL)),
            _layer_slab(layer, (D_MODEL, D_MODEL)),
            _resident((1, D_MODEL)),
        ],
        out_specs=pl.BlockSpec((tb, D_MODEL), row),
        out_shape=jax.ShapeDtypeStruct((bsz * seq, D_MODEL), F32),
        scratch_shapes=[
            pltpu.VMEM((HGRN_HEADS, HGRN_DK, HGRN_DK), F32),
            pltpu.VMEM((V7X_SUBLANES + tb, CONV_WIDTH), F32),
            pltpu.VMEM((tb, HGRN_WIDTH), F32),
        ],
        compiler_params=pltpu.CompilerParams(
            dimension_semantics=("arbitrary", "arbitrary"),
            vmem_limit_bytes=_vmem_limit(block_bytes, resident_bytes, temp_bytes)),
        name="hgrn_merge_final" if final else "hgrn_merge",
    )(x2, norm_w, w_b, lower_bounds, head_w, w_m, yc, conv_a, b_gate, wa, wb, wc, wo, final_w)


PREP_ROWS = 128
PREP_COLS = 512
PREP_NB, PREP_NC, PREP_NM = WB_COLS // PREP_COLS, -(-WC_COLS // PREP_COLS), WM_COLS // PREP_COLS


def _prep_sources():
    span = lambda start, width: [start + i * PREP_COLS for i in range(width // PREP_COLS)]
    return (span(SRC_B, WB_COLS)
            + span(SRC_CQKV, GDN_CONV_WIDTH) + span(SRC_CZ, GDN_V_WIDTH) + [SRC_SMALL]
            + span(SRC_A, 4 * CONV_WIDTH) + span(SRC_G, 3 * D_MODEL))


def _prep_in_kernel(src_ref, wt_ref, pb_ref, pc_ref, pm_ref):
    del src_ref
    j = pl.program_id(1)
    lane = lax.broadcasted_iota(jnp.int32, (D_MODEL, PREP_COLS), 1)
    pad = jnp.logical_and(j == PREP_NB + PREP_NC - 1, lane >= 2 * GDN_V_HEADS)
    blk = jnp.where(pad, 0.0, wt_ref[0].T).astype(BF16)

    @pl.when(j < PREP_NB)
    def _():
        pb_ref[...] = blk

    @pl.when(jnp.logical_and(j >= PREP_NB, j < PREP_NB + PREP_NC))
    def _():
        pc_ref[...] = blk

    @pl.when(j >= PREP_NB + PREP_NC)
    def _():
        pm_ref[...] = blk


def _prep_in_weights(w_in):
    w_t = jnp.swapaxes(w_in, 1, 2)
    src = jnp.asarray([c // V7X_SUBLANES for c in _prep_sources()], jnp.int32)
    out_spec = lambda first, count: pl.BlockSpec(
        (None, D_MODEL, PREP_COLS), lambda l, j, src_ref: (l, 0, jnp.clip(j - first, 0, count - 1)))
    out = lambda cols: jax.ShapeDtypeStruct((DEPTH, D_MODEL, cols), BF16)
    block_bytes = PREP_COLS * D_MODEL * (4 + 3 * 2)
    return pl.pallas_call(
        _prep_in_kernel,
        grid_spec=pltpu.PrefetchScalarGridSpec(
            num_scalar_prefetch=1,
            grid=(DEPTH, PREP_NB + PREP_NC + PREP_NM),
            in_specs=[pl.BlockSpec((pl.Element(1), pl.Element(PREP_COLS), pl.Element(D_MODEL)),
                                   lambda l, j, src_ref: (l, src_ref[j] * V7X_SUBLANES, 0))],
            out_specs=[out_spec(0, PREP_NB), out_spec(PREP_NB, PREP_NC), out_spec(PREP_NB + PREP_NC, PREP_NM)],
        ),
        out_shape=[out(WB_COLS), out(WC_COLS), out(WM_COLS)],
        compiler_params=pltpu.CompilerParams(
            dimension_semantics=("arbitrary", "arbitrary"),
            vmem_limit_bytes=_vmem_limit(block_bytes, 0, 4 * PREP_COLS * D_MODEL * 4)),
        name="prep_in_weights",
    )(src, w_t)


def _prep_out_kernel(wa_ref, wb_ref, wc_ref, wo_ref, oa_ref, ob_ref, oc_ref, oo_ref):
    oa_ref[...] = wa_ref[...].astype(BF16)
    ob_ref[...] = wb_ref[...].astype(BF16)
    oc_ref[...] = wc_ref[...].astype(BF16)
    oo_ref[...] = wo_ref[...].astype(BF16)


def _prep_out_weights(w_out_a, w_out_b, w_out_c, w_o):
    steps = D_MODEL // PREP_ROWS
    rows = (CONV_WIDTH, HGRN_WIDTH, GDN_V_WIDTH, D_MODEL)
    slab = lambda r: pl.BlockSpec((None, r // steps, D_MODEL), lambda l, i: (l, i, 0))
    return pl.pallas_call(
        _prep_out_kernel,
        grid=(DEPTH, steps),
        in_specs=[slab(r) for r in rows],
        out_specs=[slab(r) for r in rows],
        out_shape=[jax.ShapeDtypeStruct((DEPTH, r, D_MODEL), BF16) for r in rows],
        compiler_params=pltpu.CompilerParams(
            dimension_semantics=("arbitrary", "arbitrary"),
            vmem_limit_bytes=_vmem_limit(sum(rows) // steps * D_MODEL * 6, 0, 0)),
        name="prep_out_weights",
    )(w_out_a, w_out_b, w_out_c, w_o)


def kernel(x, norm_w, w_in, b_gate, conv_a, conv_c, a_log, dt_bias, lower_bounds, hgrn_norm_w, gdn_norm_w,
           w_out_a, w_out_b, w_out_c, w_o, final_norm_w):
    bsz, seq, _ = x.shape
    x2 = x.reshape(bsz * seq, D_MODEL)
    w_b, w_c, w_m = _prep_in_weights(w_in)
    wa, wb, wc, wo = _prep_out_weights(w_out_a, w_out_b, w_out_c, w_o)
    pad8 = lambda a: jnp.pad(a.astype(F32), ((0, 0), (GDN_V_HEADS, SMALL_COLS - 2 * GDN_V_HEADS)))
    alog_pad, dtb_pad = pad8(a_log), pad8(dt_bias)
    final_w = final_norm_w[None, :]
    for l in range(DEPTH):
        yc = _gdn(x2, norm_w, w_c, conv_c, alog_pad, dtb_pad, gdn_norm_w, layer=l, bsz=bsz, seq=seq)
        x2 = _hgrn_merge(x2, norm_w, w_b, lower_bounds, hgrn_norm_w, w_m, yc, conv_a, b_gate, wa, wb, wc, wo,
                         final_w, layer=l, final=(l == DEPTH - 1), bsz=bsz, seq=seq)
    return x2.reshape(bsz, seq, D_MODEL)
```

```python
import functools

import jax
import jax.numpy as jnp
from jax import lax
from jax.experimental import pallas as pl
from jax.experimental.pallas import tpu as pltpu

F32 = jnp.float32
BF16 = jnp.bfloat16

D_MODEL = 1024
DEPTH = 2
CHUNK = 64
NORM_EPS = 1e-6
L2_EPS = 1e-6
MIN_F = 1e-30
CONV_WIDTH = 512
CONV_K = 3
HGRN_HEADS = 4
HGRN_DK = 128
HGRN_WIDTH = HGRN_HEADS * HGRN_DK
GDN_QK_HEADS = 4
GDN_V_HEADS = 8
GDN_DK = 128
GDN_DV = 128
GDN_CONV_K = 4
GDN_QK_WIDTH = GDN_QK_HEADS * GDN_DK
GDN_V_WIDTH = GDN_V_HEADS * GDN_DV
GDN_CONV_WIDTH = 2 * GDN_QK_WIDTH + GDN_V_WIDTH
GDN_GROUPS = 2

SRC_A = 0
SRC_B = SRC_A + 4 * CONV_WIDTH
SRC_CQKV = SRC_B + 4 * HGRN_WIDTH
SRC_SMALL = SRC_CQKV + GDN_CONV_WIDTH
SRC_CZ = SRC_SMALL + 2 * GDN_V_HEADS
SRC_G = SRC_CZ + GDN_V_WIDTH
SMALL_COLS = 128

WB_COLS = 4 * HGRN_WIDTH
WC_Q, WC_K, WC_V, WC_Z, WC_SMALL = 0, 512, 1024, 2048, 3072
WC_COLS = WC_SMALL + SMALL_COLS
WM_A, WM_G = 0, 4 * CONV_WIDTH
WM_COLS = WM_G + 3 * D_MODEL

V7X_SUBLANES = 8
V7X_VMEM_BYTES = 64 * 1024 * 1024


def _vmem_limit(block_bytes, resident_bytes, temp_bytes):
    need = 2 * block_bytes + resident_bytes + temp_bytes
    return int(min(V7X_VMEM_BYTES - 8 * 1024 * 1024, need + need // 4 + 4 * 1024 * 1024))


def _resident(shape):
    return pl.BlockSpec(shape, lambda b, s: (0,) * len(shape), pipeline_mode=pl.Buffered(1))


def _layer_rows(width):
    return _resident((DEPTH, width))


def _layer_slab(layer, shape):
    return pl.BlockSpec((None,) + tuple(shape), lambda b, s: (layer,) + (0,) * len(shape),
                        pipeline_mode=pl.Buffered(1))


def _sigmoid(x):
    return 0.5 * jnp.tanh(0.5 * x) + 0.5


def _silu(x):
    return x * _sigmoid(x)


def _softplus(x):
    return jnp.maximum(x, 0.0) + jnp.log(1.0 + jnp.exp(-jnp.abs(x)))


def _dot(a, b):
    return jnp.dot(a.astype(BF16), b.astype(BF16), preferred_element_type=F32)


def _dot_nt(a, b):
    return jnp.dot(a.astype(BF16), b.T.astype(BF16), preferred_element_type=F32)


def _rmsnorm_bf16(x, w):
    ms = jnp.mean(x * x, axis=-1, keepdims=True)
    return (x * lax.rsqrt(ms + NORM_EPS) * w).astype(BF16)


def _proj(h, w_ref, start, width):
    return jnp.dot(h, w_ref[:, start:start + width], preferred_element_type=F32)


def _causal_conv(x, pad_ref, w, width):
    n = x.shape[0]
    pad_ref[V7X_SUBLANES:V7X_SUBLANES + n, :] = x
    acc = x * w[width - 1:width, :]
    for j in range(1, width):
        acc = acc + pad_ref[V7X_SUBLANES - j:V7X_SUBLANES - j + n, :] * w[width - 1 - j:width - j, :]
    pad_ref[0:V7X_SUBLANES, :] = x[n - V7X_SUBLANES:n, :]
    return acc


def _chunk_cumsum_rows(x):
    pos = lax.broadcasted_iota(jnp.int32, x.shape, 0) & (CHUNK - 1)
    d = 1
    while d < CHUNK:
        xr = pltpu.roll(x, d, 0)
        x = x + jnp.where(pos >= d, xr, 0.0)
        d *= 2
    return x


def _chunk_last_rows(x):
    n = x.shape[0]
    parts = [jnp.broadcast_to(x[c * CHUNK + CHUNK - 1:(c + 1) * CHUNK, :], (CHUNK, x.shape[1]))
             for c in range(n // CHUNK)]
    return jnp.concatenate(parts, axis=0)


def _head_rmsnorm(o, w):
    return o * lax.rsqrt(jnp.mean(o * o, axis=-1, keepdims=True) + NORM_EPS) * w


HGRN_LEVELS = (32, 16, 8, 4, 2)
HGRN_GROUPS = 1


def _hgrn_masks():
    n = CHUNK
    rowi = lax.broadcasted_iota(jnp.int32, (n, n), 0)
    coli = lax.broadcasted_iota(jnp.int32, (n, n), 1)
    levels = []
    for half in HGRN_LEVELS:
        blk = 2 * half
        levels.append(((rowi & -blk) == (coli & -blk)) & ((rowi & (blk - 1)) >= half) & ((coli & (blk - 1)) < half))
    row2 = lax.broadcasted_iota(jnp.int32, (n, 2 * n), 0)
    lane2 = lax.broadcasted_iota(jnp.int32, (n, 2 * n), 1)
    pair = (lane2 == row2) | ((lane2 == row2 + (n - 1)) & ((row2 & 1) == 1))
    sub = lax.broadcasted_iota(jnp.int32, (n, HGRN_DK), 0) & (V7X_SUBLANES - 1)
    return levels, pair, sub


def _level_ref_rows(g, half, sub):
    n = CHUNK
    g3 = g.reshape(n // 8, 8, HGRN_DK)
    bcast8 = lambda r: jnp.broadcast_to(g3[:, r:r + 1, :], g3.shape).reshape(n, HGRN_DK)
    if half >= 8:
        blk = 2 * half
        nb = n // blk
        return jnp.broadcast_to(g.reshape(nb, blk, HGRN_DK)[:, half:half + 1, :],
                                (nb, blk, HGRN_DK)).reshape(n, HGRN_DK)
    if half == 4:
        return bcast8(4)
    return jnp.where(sub < 4, bcast8(2), bcast8(6))


def _hgrn_block(h, w_ref, lb_ref, hw_ref, st_ref, yb_ref, *, layer, tb, between=()):
    n_slots = 4 + HGRN_GROUPS * (len(HGRN_LEVELS) + 1) + tb // CHUNK
    plan = [[] for _ in range(n_slots)]
    for i, work in enumerate(between):
        plan[i * n_slots // len(between)].append(work)

    def run_between():
        for work in plan.pop(0):
            work()

    rows = [lb_ref[i:i + 1, :] for i in range(DEPTH)]
    mx = functools.reduce(jnp.maximum, rows)
    es = [jnp.exp(r - mx) for r in rows]
    tot = functools.reduce(lambda a_, b_: a_ + b_, es)
    lb = jnp.zeros_like(tot)
    for i in range(1, layer + 1):
        lb = lb + es[i] / tot

    f_gate = lb + (1.0 - lb) * _sigmoid(_proj(h, w_ref, HGRN_WIDTH, HGRN_WIDTH))
    f_floor = jnp.maximum(f_gate, MIN_F)
    run_between()
    g_all = _chunk_cumsum_rows(jnp.log(f_floor))
    k_all = 1.0 - f_gate
    kp_all = k_all * pltpu.roll(f_floor, tb - 1, 0)
    run_between()
    q_all = _silu(_proj(h, w_ref, 0, HGRN_WIDTH)) * (HGRN_DK ** -0.5)
    v_all = _proj(h, w_ref, 2 * HGRN_WIDTH, HGRN_WIDTH)
    run_between()
    g_last_all = _chunk_last_rows(g_all)
    qi_all = q_all * jnp.exp(g_all)
    ks_all = k_all * jnp.exp(g_last_all - g_all)
    dec_all = jnp.exp(g_last_all)
    run_between()

    n = CHUNK
    nc = tb // CHUNK
    level_masks, pair_mask, sub = _hgrn_masks()
    tiles = [(slice(c * CHUNK, (c + 1) * CHUNK), slice(hd * HGRN_DK, (hd + 1) * HGRN_DK))
             for hd in range(HGRN_HEADS) for c in range(nc)]
    intra, upd = {}, {}
    per_group = len(tiles) // HGRN_GROUPS
    for grp in range(HGRN_GROUPS):
        ids = range(grp * per_group, (grp + 1) * per_group)
        qs = [q_all[tiles[i]] for i in ids]
        ks = [k_all[tiles[i]] for i in ids]
        gs = [g_all[tiles[i]] for i in ids]
        vs = [v_all[tiles[i]] for i in ids]
        scores = [None] * per_group
        for half, mask in zip(HGRN_LEVELS, level_masks):
            es = [jnp.exp(-jnp.abs(g - _level_ref_rows(g, half, sub))) for g in gs]
            ss = [_dot_nt(q * e, k * e) for q, k, e in zip(qs, ks, es)]
            scores = [jnp.where(mask, s, 0.0 if a is None else a) for s, a in zip(ss, scores)]
            run_between()
        s01 = [jnp.where(pair_mask, _dot_nt(q, jnp.concatenate([k, kp_all[tiles[i]]], axis=0)), 0.0)
               for q, k, i in zip(qs, ks, ids)]
        scores = [a + (s + pltpu.roll(s, n, 1))[:, :n] for a, s in zip(scores, s01)]
        for i, a, v in zip(ids, scores, vs):
            intra[i] = _dot(a, v)
            upd[i] = _dot(v.T, ks_all[tiles[i]])
        run_between()

    zz = _proj(h, w_ref, 3 * HGRN_WIDTH, HGRN_WIDTH)
    hw = hw_ref[layer:layer + 1, :]
    sts = [st_ref[hd] for hd in range(HGRN_HEADS)]
    for c in range(nc):
        for hd in range(HGRN_HEADS):
            i = hd * nc + c
            rs, cs = tiles[i]
            o = intra[i] + _dot_nt(qi_all[rs, cs], sts[hd])
            sts[hd] = dec_all[c * CHUNK:c * CHUNK + 1, cs] * sts[hd] + upd[i]
            yb_ref[rs, cs] = _head_rmsnorm(o, hw) * _silu(zz[rs, cs])
        run_between()
    for hd in range(HGRN_HEADS):
        st_ref[hd] = sts[hd]
    assert not plan


def _l2norm_heads(x, heads, width):
    outs = []
    for hd in range(heads):
        xh = x[:, hd * width:(hd + 1) * width]
        outs.append(xh * lax.rsqrt(jnp.sum(xh * xh, axis=-1, keepdims=True) + L2_EPS))
    return outs


def _unit_lower_inverse_rows(p2, masks):
    upper, blk16, blk32, eye_lower, eye_upper = masks
    n = CHUNK
    zero_eye = eye_upper.astype(BF16)
    cs = [jnp.where(blk16 & ~upper, p, 0.0) + eye_upper for p in p2]
    for _ in range(4):
        cbs = [c.astype(BF16) for c in cs]
        cs = [jnp.dot(cb, jnp.concatenate([cb, zero_eye], axis=0), preferred_element_type=F32) for cb in cbs]
        yield
    for off in (blk32 & ~blk16, ~blk32):
        ys = [_dot(jnp.where(off, p, 0.0)[:, :n], c) for p, c in zip(p2, cs)]
        yield
        cs = [_dot(c + eye_lower, jnp.concatenate([c, y], axis=0)) for c, y in zip(cs, ys)]
        yield
    return cs


def _interleave(*gens):
    live = list(gens)
    while live:
        for g in list(live):
            try:
                next(g)
            except StopIteration:
                live.remove(g)


def _conv_rows(pad_ref, w, rows, width):
    base = V7X_SUBLANES + rows.start
    n = rows.stop - rows.start
    acc = pad_ref[base:base + n, :] * w[width - 1:width, :]
    for j in range(1, width):
        acc = acc + pad_ref[base - j:base - j + n, :] * w[width - 1 - j:width - j, :]
    return acc


def _gdn_kernel(x_ref, nw_ref, w_ref, cw_ref, alog_ref, dtb_ref, hw_ref,
                y_ref, st_ref, cq_ref, ck_ref, cv_ref, *, layer, tb):
    @pl.when(pl.program_id(1) == 0)
    def _():
        st_ref[...] = jnp.zeros_like(st_ref)
        for pad_ref in (cq_ref, ck_ref, cv_ref):
            pad_ref[0:V7X_SUBLANES, :] = jnp.zeros((V7X_SUBLANES, pad_ref.shape[1]), F32)

    n = CHUNK
    nc = tb // CHUNK
    rep = GDN_V_HEADS // GDN_QK_HEADS
    h = _rmsnorm_bf16(x_ref[...], nw_ref[layer:layer + 1, :])
    cw = cw_ref[...]
    cq_ref[V7X_SUBLANES:, :] = _proj(h, w_ref, WC_Q, GDN_QK_WIDTH)
    ck_ref[V7X_SUBLANES:, :] = _proj(h, w_ref, WC_K, GDN_QK_WIDTH)
    cv_ref[V7X_SUBLANES:, :] = _proj(h, w_ref, WC_V, GDN_V_WIDTH)
    sm = _proj(h, w_ref, WC_SMALL, SMALL_COLS)
    zz = _proj(h, w_ref, WC_Z, GDN_V_WIDTH)
    alog = alog_ref[layer:layer + 1, :]
    dtb = dtb_ref[layer:layer + 1, :]
    hw = hw_ref[layer:layer + 1, :]

    row2 = lax.broadcasted_iota(jnp.int32, (n, 2 * n), 0)
    lane2 = lax.broadcasted_iota(jnp.int32, (n, 2 * n), 1)
    col2 = lane2 & (n - 1)
    upper = lane2 >= n
    strict2 = row2 > col2
    causal = lax.broadcasted_iota(jnp.int32, (n, n), 0) >= lax.broadcasted_iota(jnp.int32, (n, n), 1)
    blk16 = (row2 >> 4) == (col2 >> 4)
    blk32 = (row2 >> 5) == (col2 >> 5)
    eye_lower = jnp.where((row2 == col2) & ~upper, 1.0, 0.0).astype(F32)
    eye_upper = jnp.where((row2 == col2) & upper, 1.0, 0.0).astype(F32)
    masks = (upper, blk16, blk32, eye_lower, eye_upper)
    lane = lax.broadcasted_iota(jnp.int32, (n, SMALL_COLS), 1)
    zeros_rhs = jnp.zeros((n, GDN_DV + GDN_DK), F32)

    def local_stage(c):
        rs = slice(c * CHUNK, (c + 1) * CHUNK)
        qc = _silu(_conv_rows(cq_ref, cw[:, 0:GDN_QK_WIDTH], rs, GDN_CONV_K))
        kc = _silu(_conv_rows(ck_ref, cw[:, GDN_QK_WIDTH:2 * GDN_QK_WIDTH], rs, GDN_CONV_K))
        vc = _silu(_conv_rows(cv_ref, cw[:, 2 * GDN_QK_WIDTH:], rs, GDN_CONV_K))
        qs = [x * (GDN_DK ** -0.5) for x in _l2norm_heads(qc, GDN_QK_HEADS, GDN_DK)]
        ks = _l2norm_heads(kc, GDN_QK_HEADS, GDN_DK)
        smc = sm[rs]
        beta = _sigmoid(smc)
        g = _chunk_cumsum_rows(-jnp.exp(alog) * _softplus(smc + dtb))
        comb_t = jnp.where(lane < GDN_V_HEADS, beta, g).T
        g_last = g[n - 1:n, :]
        eg_all = jnp.exp(g)
        kfac_all = pltpu.roll(beta, GDN_V_HEADS, 1) * jnp.exp(g_last - g)
        dec_all = jnp.exp(g_last)
        out = dict(p2=[], aqk=[], rhs=[], qi=[], kst=[], dec=[])
        yield
        for hq in range(GDN_QK_HEADS):
            qh, kh = qs[hq], ks[hq]
            kq2 = _dot_nt(jnp.concatenate([-kh, qh], axis=0), jnp.concatenate([kh, kh], axis=0))
            nkk2 = kq2[:n]
            qk = kq2[n:, :n]
            for r in range(rep):
                vh = hq * rep + r
                gl = GDN_V_HEADS + vh
                g_row = comb_t[gl:gl + 1, :]
                b_row = comb_t[vh:vh + 1, :]
                g_row2 = jnp.concatenate([g_row, g_row], axis=1)
                b_row2 = jnp.concatenate([b_row, b_row], axis=1)
                dec2 = jnp.exp(jnp.minimum(g[:, gl:gl + 1] - g_row2, 0.0)) * b_row2
                eg = eg_all[:, gl:gl + 1]
                out["p2"].append(jnp.where(strict2, nkk2 * dec2, 0.0))
                out["aqk"].append(jnp.where(causal, qk * dec2[:, :n], 0.0))
                out["rhs"].append(jnp.concatenate([vc[:, vh * GDN_DV:(vh + 1) * GDN_DV], kh * eg], axis=1))
                out["qi"].append(qh * eg)
                out["kst"].append(kh * kfac_all[:, gl:gl + 1])
                out["dec"].append(dec_all[:, gl:gl + 1])
            yield
        locs[c] = out

    def solve_stage(loc):
        cs = yield from _unit_lower_inverse_rows(loc["p2"], masks)
        loc["sol"] = [_dot(c_, jnp.concatenate([zeros_rhs, rhs], axis=0)) for c_, rhs in zip(cs, loc["rhs"])]

    def state_stage(c, loc, sts):
        rs = slice(c * CHUNK, (c + 1) * CHUNK)
        wq = [_dot(jnp.concatenate([sol[:, GDN_DV:], qi], axis=0), st) for sol, qi, st in zip(loc["sol"], loc["qi"], sts)]
        es = [sol[:, :GDN_DV] - r[:n] for sol, r in zip(loc["sol"], wq)]
        ae = [_dot(jnp.concatenate([aqk, kst.T], axis=0), e) for aqk, kst, e in zip(loc["aqk"], loc["kst"], es)]
        os_ = [r[n:] + a[:n] for r, a in zip(wq, ae)]
        new = [dec * st + a[n:] for dec, st, a in zip(loc["dec"], sts, ae)]
        for vh in range(GDN_V_HEADS):
            vs = slice(vh * GDN_DV, (vh + 1) * GDN_DV)
            y_ref[rs, vs] = (_head_rmsnorm(os_[vh], hw) * _silu(zz[rs, vs])).astype(BF16)
        return new

    sts = [st_ref[vh] for vh in range(GDN_V_HEADS)]
    locs = {}
    per_group = nc // GDN_GROUPS
    groups = [range(i * per_group, (i + 1) * per_group) for i in range(GDN_GROUPS)]

    def solve_group(chunks):
        merged = {key: [v for c in chunks for v in locs[c][key]] for key in ("p2", "rhs")}
        yield from solve_stage(merged)
        for i, c in enumerate(chunks):
            locs[c]["sol"] = merged["sol"][i * GDN_V_HEADS:(i + 1) * GDN_V_HEADS]

    def local_group(chunks):
        for c in chunks:
            yield from local_stage(c)

    def state_group(chunks):
        nonlocal sts
        for c in chunks:
            sts = state_stage(c, locs.pop(c), sts)
            yield

    for i in range(GDN_GROUPS + 2):
        work = []
        if 0 <= i - 1 < GDN_GROUPS:
            work.append(solve_group(groups[i - 1]))
        if i < GDN_GROUPS:
            work.append(local_group(groups[i]))
        if 0 <= i - 2 < GDN_GROUPS:
            work.append(state_group(groups[i - 2]))
        _interleave(*work)
    for vh in range(GDN_V_HEADS):
        st_ref[vh] = sts[vh]
    for pad_ref in (cq_ref, ck_ref, cv_ref):
        pad_ref[0:V7X_SUBLANES, :] = pad_ref[tb:tb + V7X_SUBLANES, :]


def _gdn(x2, norm_w, w_c, conv_c, alog_pad, dtb_pad, head_w, *, layer, bsz, seq, tb=512):
    nblk = seq // tb
    row = lambda b, s: (b * nblk + s, 0)
    block_bytes = tb * D_MODEL * 4 + tb * GDN_V_WIDTH * 2
    resident_bytes = (D_MODEL * WC_COLS * 2 + GDN_CONV_K * GDN_CONV_WIDTH * 4
                      + (GDN_V_HEADS * GDN_DK * GDN_DV + (V7X_SUBLANES + tb) * GDN_CONV_WIDTH) * 4)
    temp_bytes = 6 * tb * WC_COLS * 4
    return pl.pallas_call(
        functools.partial(_gdn_kernel, layer=layer, tb=tb),
        grid=(bsz, nblk),
        in_specs=[
            pl.BlockSpec((tb, D_MODEL), row),
            _layer_rows(D_MODEL),
            _layer_slab(layer, (D_MODEL, WC_COLS)),
            _layer_slab(layer, (GDN_CONV_K, GDN_CONV_WIDTH)),
            _layer_rows(SMALL_COLS),
            _layer_rows(SMALL_COLS),
            _layer_rows(GDN_DV),
        ],
        out_specs=pl.BlockSpec((tb, GDN_V_WIDTH), row),
        out_shape=jax.ShapeDtypeStruct((bsz * seq, GDN_V_WIDTH), BF16),
        scratch_shapes=[
            pltpu.VMEM((GDN_V_HEADS, GDN_DK, GDN_DV), F32),
            pltpu.VMEM((V7X_SUBLANES + tb, GDN_QK_WIDTH), F32),
            pltpu.VMEM((V7X_SUBLANES + tb, GDN_QK_WIDTH), F32),
            pltpu.VMEM((V7X_SUBLANES + tb, GDN_V_WIDTH), F32),
        ],
        compiler_params=pltpu.CompilerParams(
            dimension_semantics=("arbitrary", "arbitrary"),
            vmem_limit_bytes=_vmem_limit(block_bytes, resident_bytes, temp_bytes)),
        name="gdn",
    )(x2, norm_w, w_c, conv_c, alog_pad, dtb_pad, head_w)


def _hgrn_merge_kernel(x_ref, nw_ref, wb_ref, lb_ref, hw_ref, wm_ref, yc_ref, ca_ref, bg_ref,
                       wa_ref, wbo_ref, wc_ref, wo_ref, fw_ref,
                       out_ref, st_ref, pad_ref, yb_ref, *, layer, final, tb):
    @pl.when(pl.program_id(1) == 0)
    def _():
        st_ref[...] = jnp.zeros_like(st_ref)
        pad_ref[0:V7X_SUBLANES, :] = jnp.zeros((V7X_SUBLANES, CONV_WIDTH), F32)

    x = x_ref[...]
    h = _rmsnorm_bf16(x, nw_ref[layer:layer + 1, :])
    seg = lambda i: _proj(h, wm_ref, WM_A + i * CONV_WIDTH, CONV_WIDTH)
    bg = bg_ref[layer:layer + 1, :]
    gate = lambda i: _sigmoid(_proj(h, wm_ref, WM_G + i * D_MODEL, D_MODEL) + bg[:, i * D_MODEL:(i + 1) * D_MODEL])
    part = {}

    def conv_a():
        part["conv"] = _causal_conv(seg(1) * seg(2), pad_ref, ca_ref[...], CONV_K)

    def act_a():
        part["ya"] = (seg(0) * part["conv"] * _silu(seg(3))).astype(BF16)

    def branch_a():
        part["ya2"] = jnp.dot(part["ya"], wa_ref[...], preferred_element_type=F32)

    def branch_c():
        part["yc2"] = jnp.dot(yc_ref[...], wc_ref[...], preferred_element_type=F32)

    def gate_a():
        part["merged"] = gate(0) * part["ya2"]

    def gate_c():
        part["merged"] = part["merged"] + gate(2) * part["yc2"]

    def gate_b():
        part["gate_b"] = gate(1)

    _hgrn_block(h, wb_ref, lb_ref, hw_ref, st_ref, yb_ref, layer=layer, tb=tb,
                between=(conv_a, act_a, branch_a, branch_c, gate_a, gate_c, gate_b))

    yb2 = jnp.dot(yb_ref[...].astype(BF16), wbo_ref[...], preferred_element_type=F32)
    merged = part["merged"] + part["gate_b"] * yb2
    out = x + jnp.dot(merged.astype(BF16), wo_ref[...], preferred_element_type=F32)
    if final:
        ms = jnp.mean(out * out, axis=-1, keepdims=True)
        out = out * lax.rsqrt(ms + NORM_EPS) * fw_ref[...]
    out_ref[...] = out


def _hgrn_merge(x2, norm_w, w_b, lower_bounds, head_w, w_m, yc, conv_a, b_gate, wa, wb, wc, wo, final_w,
                *, layer, final, bsz, seq, tb=512):
    nblk = seq // tb
    row = lambda b, s: (b * nblk + s, 0)
    block_bytes = tb * (2 * D_MODEL * 4 + GDN_V_WIDTH * 2)
    resident_bytes = ((D_MODEL * (WB_COLS + WM_COLS) + (CONV_WIDTH + HGRN_WIDTH + GDN_V_WIDTH + D_MODEL) * D_MODEL) * 2
                      + (HGRN_HEADS * HGRN_DK * HGRN_DK + (V7X_SUBLANES + tb) * CONV_WIDTH + tb * HGRN_WIDTH) * 4)
    temp_bytes = 4 * tb * (WB_COLS + WM_COLS) * 4
    return pl.pallas_call(
        functools.partial(_hgrn_merge_kernel, layer=layer, final=final, tb=tb),
        grid=(bsz, nblk),
        in_specs=[
            pl.BlockSpec((tb, D_MODEL), row),
            _layer_rows(D_MODEL),
            _layer_slab(layer, (D_MODEL, WB_COLS)),
            _resident((DEPTH, HGRN_WIDTH)),
            _layer_rows(HGRN_DK),
            _layer_slab(layer, (D_MODEL, WM_COLS)),
            pl.BlockSpec((tb, GDN_V_WIDTH), row),
            _layer_slab(layer, (CONV_K, CONV_WIDTH)),
            _layer_rows(3 * D_MODEL),
            _layer_slab(layer, (CONV_WIDTH, D_MODEL)),
            _layer_slab(layer, (HGRN_WIDTH, D_MODEL)),
            _layer_slab(layer, (GDN_V_WIDTH, D_MODEL)),
            _layer_slab(layer, (D_MODEL, D_MODEL)),
            _resident((1, D_MODEL)),
        ],
        out_specs=pl.BlockSpec((tb, D_MODEL), row),
        out_shape=jax.ShapeDtypeStruct((bsz * seq, D_MODEL), F32),
        scratch_shapes=[
            pltpu.VMEM((HGRN_HEADS, HGRN_DK, HGRN_DK), F32),
            pltpu.VMEM((V7X_SUBLANES + tb, CONV_WIDTH), F32),
            pltpu.VMEM((tb, HGRN_WIDTH), F32),
        ],
        compiler_params=pltpu.CompilerParams(
            dimension_semantics=("arbitrary", "arbitrary"),
            vmem_limit_bytes=_vmem_limit(block_bytes, resident_bytes, temp_bytes)),
        name="hgrn_merge_final" if final else "hgrn_merge",
    )(x2, norm_w, w_b, lower_bounds, head_w, w_m, yc, conv_a, b_gate, wa, wb, wc, wo, final_w)


PREP_ROWS = 128
PREP_COLS = 512
PREP_NB, PREP_NC, PREP_NM = WB_COLS // PREP_COLS, -(-WC_COLS // PREP_COLS), WM_COLS // PREP_COLS


def _prep_sources():
    span = lambda start, width: [start + i * PREP_COLS for i in range(width // PREP_COLS)]
    return (span(SRC_B, WB_COLS)
            + span(SRC_CQKV, GDN_CONV_WIDTH) + span(SRC_CZ, GDN_V_WIDTH) + [SRC_SMALL]
            + span(SRC_A, 4 * CONV_WIDTH) + span(SRC_G, 3 * D_MODEL))


def _prep_in_kernel(src_ref, wt_ref, pb_ref, pc_ref, pm_ref):
    del src_ref
    j = pl.program_id(1)
    lane = lax.broadcasted_iota(jnp.int32, (D_MODEL, PREP_COLS), 1)
    pad = jnp.logical_and(j == PREP_NB + PREP_NC - 1, lane >= 2 * GDN_V_HEADS)
    blk = jnp.where(pad, 0.0, wt_ref[0].T).astype(BF16)

    @pl.when(j < PREP_NB)
    def _():
        pb_ref[...] = blk

    @pl.when(jnp.logical_and(j >= PREP_NB, j < PREP_NB + PREP_NC))
    def _():
        pc_ref[...] = blk

    @pl.when(j >= PREP_NB + PREP_NC)
    def _():
        pm_ref[...] = blk


def _prep_in_weights(w_in):
    w_t = jnp.swapaxes(w_in, 1, 2)
    src = jnp.asarray([c // V7X_SUBLANES for c in _prep_sources()], jnp.int32)
    out_spec = lambda first, count: pl.BlockSpec(
        (None, D_MODEL, PREP_COLS), lambda l, j, src_ref: (l, 0, jnp.clip(j - first, 0, count - 1)))
    out = lambda cols: jax.ShapeDtypeStruct((DEPTH, D_MODEL, cols), BF16)
    block_bytes = PREP_COLS * D_MODEL * (4 + 3 * 2)
    return pl.pallas_call(
        _prep_in_kernel,
        grid_spec=pltpu.PrefetchScalarGridSpec(
            num_scalar_prefetch=1,
            grid=(DEPTH, PREP_NB + PREP_NC + PREP_NM),
            in_specs=[pl.BlockSpec((pl.Element(1), pl.Element(PREP_COLS), pl.Element(D_MODEL)),
                                   lambda l, j, src_ref: (l, src_ref[j] * V7X_SUBLANES, 0))],
            out_specs=[out_spec(0, PREP_NB), out_spec(PREP_NB, PREP_NC), out_spec(PREP_NB + PREP_NC, PREP_NM)],
        ),
        out_shape=[out(WB_COLS), out(WC_COLS), out(WM_COLS)],
        compiler_params=pltpu.CompilerParams(
            dimension_semantics=("arbitrary", "arbitrary"),
            vmem_limit_bytes=_vmem_limit(block_bytes, 0, 4 * PREP_COLS * D_MODEL * 4)),
        name="prep_in_weights",
    )(src, w_t)


def _prep_out_kernel(wa_ref, wb_ref, wc_ref, wo_ref, oa_ref, ob_ref, oc_ref, oo_ref):
    oa_ref[...] = wa_ref[...].astype(BF16)
    ob_ref[...] = wb_ref[...].astype(BF16)
    oc_ref[...] = wc_ref[...].astype(BF16)
    oo_ref[...] = wo_ref[...].astype(BF16)


def _prep_out_weights(w_out_a, w_out_b, w_out_c, w_o):
    steps = D_MODEL // PREP_ROWS
    rows = (CONV_WIDTH, HGRN_WIDTH, GDN_V_WIDTH, D_MODEL)
    slab = lambda r: pl.BlockSpec((None, r // steps, D_MODEL), lambda l, i: (l, i, 0))
    return pl.pallas_call(
        _prep_out_kernel,
        grid=(DEPTH, steps),
        in_specs=[slab(r) for r in rows],
        out_specs=[slab(r) for r in rows],
        out_shape=[jax.ShapeDtypeStruct((DEPTH, r, D_MODEL), BF16) for r in rows],
        compiler_params=pltpu.CompilerParams(
            dimension_semantics=("arbitrary", "arbitrary"),
            vmem_limit_bytes=_vmem_limit(sum(rows) // steps * D_MODEL * 6, 0, 0)),
        name="prep_out_weights",
    )(w_out_a, w_out_b, w_out_c, w_o)


def kernel(x, norm_w, w_in, b_gate, conv_a, conv_c, a_log, dt_bias, lower_bounds, hgrn_norm_w, gdn_norm_w,
           w_out_a, w_out_b, w_out_c, w_o, final_norm_w):
    bsz, seq, _ = x.shape
    x2 = x.reshape(bsz * seq, D_MODEL)
    w_b, w_c, w_m = _prep_in_weights(w_in)
    wa, wb, wc, wo = _prep_out_weights(w_out_a, w_out_b, w_out_c, w_o)
    pad8 = lambda a: jnp.pad(a.astype(F32), ((0, 0), (GDN_V_HEADS, SMALL_COLS - 2 * GDN_V_HEADS)))
    alog_pad, dtb_pad = pad8(a_log), pad8(dt_bias)
    final_w = final_norm_w[None, :]
    for l in range(DEPTH):
        yc = _gdn(x2, norm_w, w_c, conv_c, alog_pad, dtb_pad, gdn_norm_w, layer=l, bsz=bsz, seq=seq)
        x2 = _hgrn_merge(x2, norm_w, w_b, lower_bounds, hgrn_norm_w, w_m, yc, conv_a, b_gate, wa, wb, wc, wo,
                         final_w, layer=l, final=(l == DEPTH - 1), bsz=bsz, seq=seq)
    return x2.reshape(bsz, seq, D_MODEL)
```

```python
import functools

import jax
import jax.numpy as jnp
from jax import lax
from jax.experimental import pallas as pl
from jax.experimental.pallas import tpu as pltpu

F32 = jnp.float32
BF16 = jnp.bfloat16

D_MODEL = 1024
DEPTH = 2
CHUNK = 64
NORM_EPS = 1e-6
L2_EPS = 1e-6
MIN_F = 1e-30
CONV_WIDTH = 512
CONV_K = 3
HGRN_HEADS = 4
HGRN_DK = 128
HGRN_WIDTH = HGRN_HEADS * HGRN_DK
GDN_QK_HEADS = 4
GDN_V_HEADS = 8
GDN_DK = 128
GDN_DV = 128
GDN_CONV_K = 4
GDN_QK_WIDTH = GDN_QK_HEADS * GDN_DK
GDN_V_WIDTH = GDN_V_HEADS * GDN_DV
GDN_CONV_WIDTH = 2 * GDN_QK_WIDTH + GDN_V_WIDTH
GDN_GROUPS = 2

SRC_A = 0
SRC_B = SRC_A + 4 * CONV_WIDTH
SRC_CQKV = SRC_B + 4 * HGRN_WIDTH
SRC_SMALL = SRC_CQKV + GDN_CONV_WIDTH
SRC_CZ = SRC_SMALL + 2 * GDN_V_HEADS
SRC_G = SRC_CZ + GDN_V_WIDTH
SMALL_COLS = 128

WB_COLS = 4 * HGRN_WIDTH
WC_Q, WC_K, WC_V, WC_Z, WC_SMALL = 0, 512, 1024, 2048, 3072
WC_COLS = WC_SMALL + SMALL_COLS
WM_A, WM_G = 0, 4 * CONV_WIDTH
WM_COLS = WM_G + 3 * D_MODEL

V7X_SUBLANES = 8
V7X_VMEM_BYTES = 64 * 1024 * 1024


def _vmem_limit(block_bytes, resident_bytes, temp_bytes):
    need = 2 * block_bytes + resident_bytes + temp_bytes
    return int(min(V7X_VMEM_BYTES - 8 * 1024 * 1024, need + need // 4 + 4 * 1024 * 1024))


def _resident(shape):
    return pl.BlockSpec(shape, lambda b, s: (0,) * len(shape), pipeline_mode=pl.Buffered(1))


def _layer_rows(width):
    return _resident((DEPTH, width))


def _layer_slab(layer, shape):
    return pl.BlockSpec((None,) + tuple(shape), lambda b, s: (layer,) + (0,) * len(shape),
                        pipeline_mode=pl.Buffered(1))


def _sigmoid(x):
    return 0.5 * jnp.tanh(0.5 * x) + 0.5


def _silu(x):
    half = 0.5 * x
    return half * jnp.tanh(half) + half


def _softplus(x):
    return jnp.maximum(x, 0.0) + jnp.log(1.0 + jnp.exp(-jnp.abs(x)))


def _dot(a, b):
    return jnp.dot(a.astype(BF16), b.astype(BF16), preferred_element_type=F32)


def _dot_nt(a, b):
    return lax.dot_general(a.astype(BF16), b.astype(BF16), (((1,), (1,)), ((), ())),
                           preferred_element_type=F32)


def _rmsnorm_bf16(x, w):
    ms = jnp.mean(x * x, axis=-1, keepdims=True)
    return (x * lax.rsqrt(ms + NORM_EPS) * w).astype(BF16)


def _proj(h, w_ref, start, width):
    return jnp.dot(h, w_ref[:, start:start + width], preferred_element_type=F32)


def _causal_conv(x, pad_ref, w, width):
    n = x.shape[0]
    pad_ref[V7X_SUBLANES:V7X_SUBLANES + n, :] = x
    acc = x * w[width - 1:width, :]
    for j in range(1, width):
        acc = acc + pad_ref[V7X_SUBLANES - j:V7X_SUBLANES - j + n, :] * w[width - 1 - j:width - j, :]
    pad_ref[0:V7X_SUBLANES, :] = x[n - V7X_SUBLANES:n, :]
    return acc


def _chunk_cumsum_rows(x):
    pos = lax.broadcasted_iota(jnp.int32, x.shape, 0) & (CHUNK - 1)
    d = 1
    while d < CHUNK:
        xr = pltpu.roll(x, d, 0)
        x = x + jnp.where(pos >= d, xr, 0.0)
        d *= 2
    return x


def _chunk_last_rows(x):
    n = x.shape[0]
    parts = [jnp.broadcast_to(x[c * CHUNK + CHUNK - 1:(c + 1) * CHUNK, :], (CHUNK, x.shape[1]))
             for c in range(n // CHUNK)]
    return jnp.concatenate(parts, axis=0)


def _head_rmsnorm(o, w):
    return o * lax.rsqrt(jnp.mean(o * o, axis=-1, keepdims=True) + NORM_EPS) * w


HGRN_LEVELS = (32, 16, 8, 4, 2)


def _hgrn_masks():
    n = CHUNK
    rowi = lax.broadcasted_iota(jnp.int32, (n, n), 0)
    coli = lax.broadcasted_iota(jnp.int32, (n, n), 1)
    levels = []
    for half in HGRN_LEVELS:
        blk = 2 * half
        levels.append(((rowi & -blk) == (coli & -blk)) & ((rowi & (blk - 1)) >= half) & ((coli & (blk - 1)) < half))
    row2 = lax.broadcasted_iota(jnp.int32, (n, 2 * n), 0)
    lane2 = lax.broadcasted_iota(jnp.int32, (n, 2 * n), 1)
    pair = (lane2 == row2) | ((lane2 == row2 + (n - 1)) & ((row2 & 1) == 1))
    sub = lax.broadcasted_iota(jnp.int32, (n, HGRN_DK), 0) & (V7X_SUBLANES - 1)
    return levels, pair, sub


def _level_ref_rows(g, half, sub):
    n = CHUNK
    g3 = g.reshape(n // 8, 8, HGRN_DK)
    bcast8 = lambda r: jnp.broadcast_to(g3[:, r:r + 1, :], g3.shape).reshape(n, HGRN_DK)
    if half >= 8:
        blk = 2 * half
        nb = n // blk
        return jnp.broadcast_to(g.reshape(nb, blk, HGRN_DK)[:, half:half + 1, :],
                                (nb, blk, HGRN_DK)).reshape(n, HGRN_DK)
    if half == 4:
        return bcast8(4)
    return jnp.where(sub < 4, bcast8(2), bcast8(6))


def _hgrn_block(h, w_ref, lb_ref, hw_ref, st_ref, yb_ref, *, layer, tb, between=()):
    n_slots = 4 + len(HGRN_LEVELS) + 1 + tb // CHUNK
    plan = [[] for _ in range(n_slots)]
    for i, work in enumerate(between):
        plan[i * n_slots // len(between)].append(work)

    def run_between():
        for work in plan.pop(0):
            work()

    rows = [lb_ref[i:i + 1, :] for i in range(DEPTH)]
    mx = functools.reduce(jnp.maximum, rows)
    es = [jnp.exp(r - mx) for r in rows]
    tot = functools.reduce(lambda a_, b_: a_ + b_, es)
    lb = jnp.zeros_like(tot)
    for i in range(1, layer + 1):
        lb = lb + es[i] / tot

    f_gate = (0.5 + 0.5 * lb) + (0.5 - 0.5 * lb) * jnp.tanh(0.5 * _proj(h, w_ref, HGRN_WIDTH, HGRN_WIDTH))
    f_floor = jnp.maximum(f_gate, MIN_F)
    run_between()
    g_all = _chunk_cumsum_rows(jnp.log(f_floor))
    k_all = 1.0 - f_gate
    kp_all = k_all * pltpu.roll(f_floor, tb - 1, 0)
    run_between()
    q_all = _silu(_proj(h, w_ref, 0, HGRN_WIDTH)) * (HGRN_DK ** -0.5)
    v_all = _proj(h, w_ref, 2 * HGRN_WIDTH, HGRN_WIDTH)
    run_between()
    g_last_all = _chunk_last_rows(g_all)
    qi_all = q_all * jnp.exp(g_all)
    ks_all = k_all * jnp.exp(g_last_all - g_all)
    dec_all = jnp.exp(g_last_all)
    run_between()

    n = CHUNK
    nc = tb // CHUNK
    level_masks, pair_mask, sub = _hgrn_masks()
    tiles = [(slice(c * CHUNK, (c + 1) * CHUNK), slice(hd * HGRN_DK, (hd + 1) * HGRN_DK))
             for hd in range(HGRN_HEADS) for c in range(nc)]
    qs = [q_all[t] for t in tiles]
    ks = [k_all[t] for t in tiles]
    gs = [g_all[t] for t in tiles]
    vs = [v_all[t] for t in tiles]
    scores = [None] * len(tiles)
    for half, mask in zip(HGRN_LEVELS, level_masks):
        es = [jnp.exp(-jnp.abs(g - _level_ref_rows(g, half, sub))) for g in gs]
        ss = [_dot_nt(q * e, k * e) for q, k, e in zip(qs, ks, es)]
        scores = [jnp.where(mask, s, 0.0 if a is None else a) for s, a in zip(ss, scores)]
        run_between()
    s01 = [jnp.where(pair_mask, _dot_nt(q, jnp.concatenate([k, kp_all[t]], axis=0)), 0.0)
           for q, k, t in zip(qs, ks, tiles)]
    scores = [a + (s + pltpu.roll(s, n, 1))[:, :n] for a, s in zip(scores, s01)]
    intra = [_dot(a, v) for a, v in zip(scores, vs)]
    upd = [_dot(v.T, ks_all[t]) for v, t in zip(vs, tiles)]
    run_between()

    zz = _proj(h, w_ref, 3 * HGRN_WIDTH, HGRN_WIDTH)
    hw = hw_ref[layer:layer + 1, :]
    sts = [st_ref[hd] for hd in range(HGRN_HEADS)]
    for c in range(nc):
        for hd in range(HGRN_HEADS):
            i = hd * nc + c
            rs, cs = tiles[i]
            o = intra[i] + _dot_nt(qi_all[rs, cs], sts[hd])
            sts[hd] = dec_all[c * CHUNK:c * CHUNK + 1, cs] * sts[hd] + upd[i]
            yb_ref[rs, cs] = _head_rmsnorm(o, hw) * _silu(zz[rs, cs])
        run_between()
    for hd in range(HGRN_HEADS):
        st_ref[hd] = sts[hd]
    assert not plan


def _l2norm_heads(x, heads, width):
    outs = []
    for hd in range(heads):
        xh = x[:, hd * width:(hd + 1) * width]
        outs.append(xh * lax.rsqrt(jnp.sum(xh * xh, axis=-1, keepdims=True) + L2_EPS))
    return outs


def _unit_lower_inverse_rows(p2, masks):
    upper, blk16, blk32, eye_lower, eye_upper = masks
    n = CHUNK
    zero_eye = eye_upper.astype(BF16)
    cs = [jnp.where(blk16 & ~upper, p, 0.0) + eye_upper for p in p2]
    for _ in range(4):
        cbs = [c.astype(BF16) for c in cs]
        cs = [jnp.dot(cb, jnp.concatenate([cb, zero_eye], axis=0), preferred_element_type=F32) for cb in cbs]
        yield
    for off in (blk32 & ~blk16, ~blk32):
        ys = [_dot(jnp.where(off, p, 0.0)[:, :n], c) for p, c in zip(p2, cs)]
        yield
        cs = [_dot(c + eye_lower, jnp.concatenate([c, y], axis=0)) for c, y in zip(cs, ys)]
        yield
    return cs


def _interleave(*gens):
    live = list(gens)
    while live:
        for g in list(live):
            try:
                next(g)
            except StopIteration:
                live.remove(g)


def _conv_rows(pad_ref, w, rows, width):
    base = V7X_SUBLANES + rows.start
    n = rows.stop - rows.start
    acc = pad_ref[base:base + n, :] * w[width - 1:width, :]
    for j in range(1, width):
        acc = acc + pad_ref[base - j:base - j + n, :] * w[width - 1 - j:width - j, :]
    return acc


def _gdn_kernel(x_ref, nw_ref, w_ref, cw_ref, alog_ref, dtb_ref, hw_ref,
                y_ref, st_ref, cq_ref, ck_ref, cv_ref, *, layer, tb):
    @pl.when(pl.program_id(1) == 0)
    def _():
        st_ref[...] = jnp.zeros_like(st_ref)
        for pad_ref in (cq_ref, ck_ref, cv_ref):
            pad_ref[0:V7X_SUBLANES, :] = jnp.zeros((V7X_SUBLANES, pad_ref.shape[1]), F32)

    n = CHUNK
    nc = tb // CHUNK
    rep = GDN_V_HEADS // GDN_QK_HEADS
    h = _rmsnorm_bf16(x_ref[...], nw_ref[layer:layer + 1, :])
    cw = cw_ref[...]
    cq_ref[V7X_SUBLANES:, :] = _proj(h, w_ref, WC_Q, GDN_QK_WIDTH)
    ck_ref[V7X_SUBLANES:, :] = _proj(h, w_ref, WC_K, GDN_QK_WIDTH)
    cv_ref[V7X_SUBLANES:, :] = _proj(h, w_ref, WC_V, GDN_V_WIDTH)
    sm = _proj(h, w_ref, WC_SMALL, SMALL_COLS)
    zz = _proj(h, w_ref, WC_Z, GDN_V_WIDTH)
    alog = alog_ref[layer:layer + 1, :]
    dtb = dtb_ref[layer:layer + 1, :]
    hw = hw_ref[layer:layer + 1, :]

    row2 = lax.broadcasted_iota(jnp.int32, (n, 2 * n), 0)
    lane2 = lax.broadcasted_iota(jnp.int32, (n, 2 * n), 1)
    col2 = lane2 & (n - 1)
    upper = lane2 >= n
    strict2 = row2 > col2
    causal = lax.broadcasted_iota(jnp.int32, (n, n), 0) >= lax.broadcasted_iota(jnp.int32, (n, n), 1)
    blk16 = (row2 >> 4) == (col2 >> 4)
    blk32 = (row2 >> 5) == (col2 >> 5)
    eye_lower = jnp.where((row2 == col2) & ~upper, 1.0, 0.0).astype(F32)
    eye_upper = jnp.where((row2 == col2) & upper, 1.0, 0.0).astype(F32)
    masks = (upper, blk16, blk32, eye_lower, eye_upper)
    lane = lax.broadcasted_iota(jnp.int32, (n, SMALL_COLS), 1)
    zeros_rhs = jnp.zeros((n, GDN_DV + GDN_DK), F32)

    def local_stage(c):
        rs = slice(c * CHUNK, (c + 1) * CHUNK)
        qc = _silu(_conv_rows(cq_ref, cw[:, 0:GDN_QK_WIDTH], rs, GDN_CONV_K))
        kc = _silu(_conv_rows(ck_ref, cw[:, GDN_QK_WIDTH:2 * GDN_QK_WIDTH], rs, GDN_CONV_K))
        vc = _silu(_conv_rows(cv_ref, cw[:, 2 * GDN_QK_WIDTH:], rs, GDN_CONV_K))
        qs = [x * (GDN_DK ** -0.5) for x in _l2norm_heads(qc, GDN_QK_HEADS, GDN_DK)]
        ks = _l2norm_heads(kc, GDN_QK_HEADS, GDN_DK)
        smc = sm[rs]
        beta = _sigmoid(smc)
        g = _chunk_cumsum_rows(-jnp.exp(alog) * _softplus(smc + dtb))
        comb_t = jnp.where(lane < GDN_V_HEADS, beta, g).T
        g_last = g[n - 1:n, :]
        eg_all = jnp.exp(g)
        kfac_all = pltpu.roll(beta, GDN_V_HEADS, 1) * jnp.exp(g_last - g)
        dec_all = jnp.exp(g_last)
        out = dict(p2=[], aqk=[], rhs=[], qi=[], kst=[], dec=[])
        yield
        for hq in range(GDN_QK_HEADS):
            qh, kh = qs[hq], ks[hq]
            kq2 = _dot_nt(jnp.concatenate([-kh, qh], axis=0), jnp.concatenate([kh, kh], axis=0))
            nkk2 = kq2[:n]
            qk = kq2[n:, :n]
            for r in range(rep):
                vh = hq * rep + r
                gl = GDN_V_HEADS + vh
                g_row = comb_t[gl:gl + 1, :]
                b_row = comb_t[vh:vh + 1, :]
                g_row2 = jnp.concatenate([g_row, g_row], axis=1)
                b_row2 = jnp.concatenate([b_row, b_row], axis=1)
                dec2 = jnp.exp(jnp.minimum(g[:, gl:gl + 1] - g_row2, 0.0)) * b_row2
                eg = eg_all[:, gl:gl + 1]
                out["p2"].append(jnp.where(strict2, nkk2 * dec2, 0.0))
                out["aqk"].append(jnp.where(causal, qk * dec2[:, :n], 0.0))
                out["rhs"].append(jnp.concatenate([vc[:, vh * GDN_DV:(vh + 1) * GDN_DV], kh * eg], axis=1))
                out["qi"].append(qh * eg)
                out["kst"].append(kh * kfac_all[:, gl:gl + 1])
                out["dec"].append(dec_all[:, gl:gl + 1])
            yield
        locs[c] = out

    def solve_stage(loc):
        cs = yield from _unit_lower_inverse_rows(loc["p2"], masks)
        loc["sol"] = [_dot(c_, jnp.concatenate([zeros_rhs, rhs], axis=0)) for c_, rhs in zip(cs, loc["rhs"])]

    def state_stage(c, loc, sts):
        rs = slice(c * CHUNK, (c + 1) * CHUNK)
        wq = [_dot(jnp.concatenate([sol[:, GDN_DV:], qi], axis=0), st) for sol, qi, st in zip(loc["sol"], loc["qi"], sts)]
        es = [sol[:, :GDN_DV] - r[:n] for sol, r in zip(loc["sol"], wq)]
        ae = [_dot(jnp.concatenate([aqk, kst.T], axis=0), e) for aqk, kst, e in zip(loc["aqk"], loc["kst"], es)]
        os_ = [r[n:] + a[:n] for r, a in zip(wq, ae)]
        new = [dec * st + a[n:] for dec, st, a in zip(loc["dec"], sts, ae)]
        for vh in range(GDN_V_HEADS):
            vs = slice(vh * GDN_DV, (vh + 1) * GDN_DV)
            y_ref[rs, vs] = (_head_rmsnorm(os_[vh], hw) * _silu(zz[rs, vs])).astype(BF16)
        return new

    sts = [st_ref[vh] for vh in range(GDN_V_HEADS)]
    locs = {}
    per_group = nc // GDN_GROUPS
    groups = [range(i * per_group, (i + 1) * per_group) for i in range(GDN_GROUPS)]

    def solve_group(chunks):
        merged = {key: [v for c in chunks for v in locs[c][key]] for key in ("p2", "rhs")}
        yield from solve_stage(merged)
        for i, c in enumerate(chunks):
            locs[c]["sol"] = merged["sol"][i * GDN_V_HEADS:(i + 1) * GDN_V_HEADS]

    def local_group(chunks):
        for c in chunks:
            yield from local_stage(c)

    def state_group(chunks):
        nonlocal sts
        for c in chunks:
            sts = state_stage(c, locs.pop(c), sts)
            yield

    for i in range(GDN_GROUPS + 2):
        work = []
        if 0 <= i - 1 < GDN_GROUPS:
            work.append(solve_group(groups[i - 1]))
        if i < GDN_GROUPS:
            work.append(local_group(groups[i]))
        if 0 <= i - 2 < GDN_GROUPS:
            work.append(state_group(groups[i - 2]))
        _interleave(*work)
    for vh in range(GDN_V_HEADS):
        st_ref[vh] = sts[vh]
    for pad_ref in (cq_ref, ck_ref, cv_ref):
        pad_ref[0:V7X_SUBLANES, :] = pad_ref[tb:tb + V7X_SUBLANES, :]


def _gdn(x2, norm_w, w_c, conv_c, alog_pad, dtb_pad, head_w, *, layer, bsz, seq, tb=512):
    nblk = seq // tb
    row = lambda b, s: (b * nblk + s, 0)
    block_bytes = tb * D_MODEL * 4 + tb * GDN_V_WIDTH * 2
    resident_bytes = (D_MODEL * WC_COLS * 2 + GDN_CONV_K * GDN_CONV_WIDTH * 4
                      + (GDN_V_HEADS * GDN_DK * GDN_DV + (V7X_SUBLANES + tb) * GDN_CONV_WIDTH) * 4)
    temp_bytes = 6 * tb * WC_COLS * 4
    return pl.pallas_call(
        functools.partial(_gdn_kernel, layer=layer, tb=tb),
        grid=(bsz, nblk),
        in_specs=[
            pl.BlockSpec((tb, D_MODEL), row),
            _layer_rows(D_MODEL),
            _layer_slab(layer, (D_MODEL, WC_COLS)),
            _layer_slab(layer, (GDN_CONV_K, GDN_CONV_WIDTH)),
            _layer_rows(SMALL_COLS),
            _layer_rows(SMALL_COLS),
            _layer_rows(GDN_DV),
        ],
        out_specs=pl.BlockSpec((tb, GDN_V_WIDTH), row),
        out_shape=jax.ShapeDtypeStruct((bsz * seq, GDN_V_WIDTH), BF16),
        scratch_shapes=[
            pltpu.VMEM((GDN_V_HEADS, GDN_DK, GDN_DV), F32),
            pltpu.VMEM((V7X_SUBLANES + tb, GDN_QK_WIDTH), F32),
            pltpu.VMEM((V7X_SUBLANES + tb, GDN_QK_WIDTH), F32),
            pltpu.VMEM((V7X_SUBLANES + tb, GDN_V_WIDTH), F32),
        ],
        compiler_params=pltpu.CompilerParams(
            dimension_semantics=("arbitrary", "arbitrary"),
            vmem_limit_bytes=_vmem_limit(block_bytes, resident_bytes, temp_bytes)),
        name="gdn",
    )(x2, norm_w, w_c, conv_c, alog_pad, dtb_pad, head_w)


def _hgrn_merge_kernel(x_ref, nw_ref, wb_ref, lb_ref, hw_ref, wm_ref, yc_ref, ca_ref, bg_ref,
                       wa_ref, wbo_ref, wc_ref, wo_ref, fw_ref,
                       out_ref, st_ref, pad_ref, yb_ref, *, layer, final, tb):
    @pl.when(pl.program_id(1) == 0)
    def _():
        st_ref[...] = jnp.zeros_like(st_ref)
        pad_ref[0:V7X_SUBLANES, :] = jnp.zeros((V7X_SUBLANES, CONV_WIDTH), F32)

    x = x_ref[...]
    h = _rmsnorm_bf16(x, nw_ref[layer:layer + 1, :])
    seg = lambda i: _proj(h, wm_ref, WM_A + i * CONV_WIDTH, CONV_WIDTH)
    half_bg = 0.5 * bg_ref[layer:layer + 1, :]
    tgate = lambda i: jnp.tanh(_proj(h, wm_ref, WM_G + i * D_MODEL, D_MODEL) + half_bg[:, i * D_MODEL:(i + 1) * D_MODEL])
    part = {}

    def conv_a():
        part["conv"] = _causal_conv(seg(1) * seg(2), pad_ref, ca_ref[...], CONV_K)

    def act_a():
        part["ya"] = (seg(0) * part["conv"] * _silu(seg(3))).astype(BF16)

    def branch_a():
        part["ya2"] = jnp.dot(part["ya"], wa_ref[...], preferred_element_type=F32)

    def branch_c():
        part["yc2"] = jnp.dot(yc_ref[...], wc_ref[...], preferred_element_type=F32)

    def gate_a():
        part["merged"] = tgate(0) * part["ya2"] + part["ya2"]

    def gate_c():
        part["merged"] = part["merged"] + (tgate(2) * part["yc2"] + part["yc2"])

    def gate_b():
        part["gate_b"] = tgate(1)

    _hgrn_block(h, wb_ref, lb_ref, hw_ref, st_ref, yb_ref, layer=layer, tb=tb,
                between=(conv_a, act_a, branch_a, branch_c, gate_a, gate_c, gate_b))

    yb2 = jnp.dot(yb_ref[...].astype(BF16), wbo_ref[...], preferred_element_type=F32)
    merged = part["merged"] + (part["gate_b"] * yb2 + yb2)
    out = x + jnp.dot(merged.astype(BF16), wo_ref[...], preferred_element_type=F32)
    if final:
        ms = jnp.mean(out * out, axis=-1, keepdims=True)
        out = out * lax.rsqrt(ms + NORM_EPS) * fw_ref[...]
    out_ref[...] = out


def _hgrn_merge(x2, norm_w, w_b, lower_bounds, head_w, w_m, yc, conv_a, b_gate, wa, wb, wc, wo, final_w,
                *, layer, final, bsz, seq, tb=512):
    nblk = seq // tb
    row = lambda b, s: (b * nblk + s, 0)
    block_bytes = tb * (2 * D_MODEL * 4 + GDN_V_WIDTH * 2)
    resident_bytes = ((D_MODEL * (WB_COLS + WM_COLS) + (CONV_WIDTH + HGRN_WIDTH + GDN_V_WIDTH + D_MODEL) * D_MODEL) * 2
                      + (HGRN_HEADS * HGRN_DK * HGRN_DK + (V7X_SUBLANES + tb) * CONV_WIDTH + tb * HGRN_WIDTH) * 4)
    temp_bytes = 4 * tb * (WB_COLS + WM_COLS) * 4
    return pl.pallas_call(
        functools.partial(_hgrn_merge_kernel, layer=layer, final=final, tb=tb),
        grid=(bsz, nblk),
        in_specs=[
            pl.BlockSpec((tb, D_MODEL), row),
            _layer_rows(D_MODEL),
            _layer_slab(layer, (D_MODEL, WB_COLS)),
            _resident((DEPTH, HGRN_WIDTH)),
            _layer_rows(HGRN_DK),
            _layer_slab(layer, (D_MODEL, WM_COLS)),
            pl.BlockSpec((tb, GDN_V_WIDTH), row),
            _layer_slab(layer, (CONV_K, CONV_WIDTH)),
            _layer_rows(3 * D_MODEL),
            _layer_slab(layer, (CONV_WIDTH, D_MODEL)),
            _layer_slab(layer, (HGRN_WIDTH, D_MODEL)),
            _layer_slab(layer, (GDN_V_WIDTH, D_MODEL)),
            _layer_slab(layer, (D_MODEL, D_MODEL)),
            _resident((1, D_MODEL)),
        ],
        out_specs=pl.BlockSpec((tb, D_MODEL), row),
        out_shape=jax.ShapeDtypeStruct((bsz * seq, D_MODEL), F32),
        scratch_shapes=[
            pltpu.VMEM((HGRN_HEADS, HGRN_DK, HGRN_DK), F32),
            pltpu.VMEM((V7X_SUBLANES + tb, CONV_WIDTH), F32),
            pltpu.VMEM((tb, HGRN_WIDTH), F32),
        ],
        compiler_params=pltpu.CompilerParams(
            dimension_semantics=("arbitrary", "arbitrary"),
            vmem_limit_bytes=_vmem_limit(block_bytes, resident_bytes, temp_bytes)),
        name="hgrn_merge_final" if final else "hgrn_merge",
    )(x2, norm_w, w_b, lower_bounds, head_w, w_m, yc, conv_a, b_gate, wa, wb, wc, wo, final_w)


PREP_ROWS = 128
PREP_COLS = 512
PREP_NB, PREP_NC, PREP_NM = WB_COLS // PREP_COLS, -(-WC_COLS // PREP_COLS), WM_COLS // PREP_COLS


def _prep_sources():
    span = lambda start, width: [start + i * PREP_COLS for i in range(width // PREP_COLS)]
    return (span(SRC_B, WB_COLS)
            + span(SRC_CQKV, GDN_CONV_WIDTH) + span(SRC_CZ, GDN_V_WIDTH) + [SRC_SMALL]
            + span(SRC_A, 4 * CONV_WIDTH) + span(SRC_G, 3 * D_MODEL))


def _prep_in_kernel(src_ref, wt_ref, pb_ref, pc_ref, pm_ref):
    del src_ref
    j = pl.program_id(1)
    small = PREP_NB + PREP_NC - 1
    gates = PREP_NB + PREP_NC + 4 * CONV_WIDTH // PREP_COLS
    blk = wt_ref[0].T

    @pl.when(j < PREP_NB)
    def _():
        pb_ref[...] = blk.astype(BF16)

    @pl.when(jnp.logical_and(j >= PREP_NB, j < small))
    def _():
        pc_ref[...] = blk.astype(BF16)

    @pl.when(j == small)
    def _():
        lane = lax.broadcasted_iota(jnp.int32, (D_MODEL, PREP_COLS), 1)
        pc_ref[...] = jnp.where(lane < 2 * GDN_V_HEADS, blk, 0.0).astype(BF16)

    @pl.when(jnp.logical_and(j > small, j < gates))
    def _():
        pm_ref[...] = blk.astype(BF16)

    @pl.when(j >= gates)
    def _():
        pm_ref[...] = (0.5 * blk).astype(BF16)


def _prep_in_weights(w_in):
    w_t = jnp.swapaxes(w_in, 1, 2)
    src = jnp.asarray([c // V7X_SUBLANES for c in _prep_sources()], jnp.int32)
    out_spec = lambda first, count: pl.BlockSpec(
        (None, D_MODEL, PREP_COLS), lambda l, j, src_ref: (l, 0, jnp.clip(j - first, 0, count - 1)))
    out = lambda cols: jax.ShapeDtypeStruct((DEPTH, D_MODEL, cols), BF16)
    block_bytes = PREP_COLS * D_MODEL * (4 + 3 * 2)
    return pl.pallas_call(
        _prep_in_kernel,
        grid_spec=pltpu.PrefetchScalarGridSpec(
            num_scalar_prefetch=1,
            grid=(DEPTH, PREP_NB + PREP_NC + PREP_NM),
            in_specs=[pl.BlockSpec((pl.Element(1), pl.Element(PREP_COLS), pl.Element(D_MODEL)),
                                   lambda l, j, src_ref: (l, src_ref[j] * V7X_SUBLANES, 0))],
            out_specs=[out_spec(0, PREP_NB), out_spec(PREP_NB, PREP_NC), out_spec(PREP_NB + PREP_NC, PREP_NM)],
        ),
        out_shape=[out(WB_COLS), out(WC_COLS), out(WM_COLS)],
        compiler_params=pltpu.CompilerParams(
            dimension_semantics=("arbitrary", "arbitrary"),
            vmem_limit_bytes=_vmem_limit(block_bytes, 0, 4 * PREP_COLS * D_MODEL * 4)),
        name="prep_in_weights",
    )(src, w_t)


def _prep_out_kernel(wa_ref, wb_ref, wc_ref, wo_ref, oa_ref, ob_ref, oc_ref, oo_ref):
    oa_ref[...] = wa_ref[...].astype(BF16)
    ob_ref[...] = wb_ref[...].astype(BF16)
    oc_ref[...] = wc_ref[...].astype(BF16)
    oo_ref[...] = (0.5 * wo_ref[...]).astype(BF16)


def _prep_out_weights(w_out_a, w_out_b, w_out_c, w_o):
    steps = D_MODEL // PREP_ROWS
    rows = (CONV_WIDTH, HGRN_WIDTH, GDN_V_WIDTH, D_MODEL)
    slab = lambda r: pl.BlockSpec((None, r // steps, D_MODEL), lambda l, i: (l, i, 0))
    return pl.pallas_call(
        _prep_out_kernel,
        grid=(DEPTH, steps),
        in_specs=[slab(r) for r in rows],
        out_specs=[slab(r) for r in rows],
        out_shape=[jax.ShapeDtypeStruct((DEPTH, r, D_MODEL), BF16) for r in rows],
        compiler_params=pltpu.CompilerParams(
            dimension_semantics=("arbitrary", "arbitrary"),
            vmem_limit_bytes=_vmem_limit(sum(rows) // steps * D_MODEL * 6, 0, 0)),
        name="prep_out_weights",
    )(w_out_a, w_out_b, w_out_c, w_o)


def kernel(x, norm_w, w_in, b_gate, conv_a, conv_c, a_log, dt_bias, lower_bounds, hgrn_norm_w, gdn_norm_w,
           w_out_a, w_out_b, w_out_c, w_o, final_norm_w):
    bsz, seq, _ = x.shape
    x2 = x.reshape(bsz * seq, D_MODEL)
    w_b, w_c, w_m = _prep_in_weights(w_in)
    wa, wb, wc, wo = _prep_out_weights(w_out_a, w_out_b, w_out_c, w_o)
    pad8 = lambda a: jnp.pad(a.astype(F32), ((0, 0), (GDN_V_HEADS, SMALL_COLS - 2 * GDN_V_HEADS)))
    alog_pad, dtb_pad = pad8(a_log), pad8(dt_bias)
    final_w = final_norm_w[None, :]
    for l in range(DEPTH):
        yc = _gdn(x2, norm_w, w_c, conv_c, alog_pad, dtb_pad, gdn_norm_w, layer=l, bsz=bsz, seq=seq)
        x2 = _hgrn_merge(x2, norm_w, w_b, lower_bounds, hgrn_norm_w, w_m, yc, conv_a, b_gate, wa, wb, wc, wo,
                         final_w, layer=l, final=(l == DEPTH - 1), bsz=bsz, seq=seq)
    return x2.reshape(bsz, seq, D_MODEL)
```

```python
import functools

import jax
import jax.numpy as jnp
from jax import lax
from jax.experimental import pallas as pl
from jax.experimental.pallas import tpu as pltpu

F32 = jnp.float32
BF16 = jnp.bfloat16

D_MODEL = 1024
DEPTH = 2
CHUNK = 64
NORM_EPS = 1e-6
L2_EPS = 1e-6
MIN_F = 1e-30
CONV_WIDTH = 512
CONV_K = 3
HGRN_HEADS = 4
HGRN_DK = 128
HGRN_WIDTH = HGRN_HEADS * HGRN_DK
GDN_QK_HEADS = 4
GDN_V_HEADS = 8
GDN_DK = 128
GDN_DV = 128
GDN_CONV_K = 4
GDN_QK_WIDTH = GDN_QK_HEADS * GDN_DK
GDN_V_WIDTH = GDN_V_HEADS * GDN_DV
GDN_CONV_WIDTH = 2 * GDN_QK_WIDTH + GDN_V_WIDTH
GDN_GROUPS = 2

SRC_A = 0
SRC_B = SRC_A + 4 * CONV_WIDTH
SRC_CQKV = SRC_B + 4 * HGRN_WIDTH
SRC_SMALL = SRC_CQKV + GDN_CONV_WIDTH
SRC_CZ = SRC_SMALL + 2 * GDN_V_HEADS
SRC_G = SRC_CZ + GDN_V_WIDTH
SMALL_COLS = 128

WB_COLS = 4 * HGRN_WIDTH
WC_Q, WC_K, WC_V, WC_Z, WC_SMALL = 0, 512, 1024, 2048, 3072
WC_COLS = WC_SMALL + SMALL_COLS
WM_A, WM_G = 0, 4 * CONV_WIDTH
WM_COLS = WM_G + 3 * D_MODEL

V7X_SUBLANES = 8
V7X_VMEM_BYTES = 64 * 1024 * 1024


def _vmem_limit(block_bytes, resident_bytes, temp_bytes):
    need = 2 * block_bytes + resident_bytes + temp_bytes
    return int(min(V7X_VMEM_BYTES - 8 * 1024 * 1024, need + need // 4 + 4 * 1024 * 1024))


def _resident(shape):
    return pl.BlockSpec(shape, lambda b, s: (0,) * len(shape), pipeline_mode=pl.Buffered(1))


def _layer_rows(width):
    return _resident((DEPTH, width))


def _layer_slab(layer, shape):
    return pl.BlockSpec((None,) + tuple(shape), lambda b, s: (layer,) + (0,) * len(shape),
                        pipeline_mode=pl.Buffered(1))


def _sigmoid(x):
    return 0.5 * jnp.tanh(0.5 * x) + 0.5


def _silu(x):
    half = 0.5 * x
    return half * jnp.tanh(half) + half


def _softplus(x):
    return jnp.maximum(x, 0.0) + jnp.log(1.0 + jnp.exp(-jnp.abs(x)))


def _dot(a, b):
    return jnp.dot(a.astype(BF16), b.astype(BF16), preferred_element_type=F32)


def _dot_nt(a, b):
    return lax.dot_general(a.astype(BF16), b.astype(BF16), (((1,), (1,)), ((), ())),
                           preferred_element_type=F32)


def _rmsnorm_bf16(x, w):
    ms = jnp.mean(x * x, axis=-1, keepdims=True)
    return (x * lax.rsqrt(ms + NORM_EPS) * w).astype(BF16)


def _proj(h, w_ref, start, width):
    return jnp.dot(h, w_ref[:, start:start + width], preferred_element_type=F32)


def _causal_conv(x, pad_ref, w, width):
    n = x.shape[0]
    pad_ref[V7X_SUBLANES:V7X_SUBLANES + n, :] = x
    acc = x * w[width - 1:width, :]
    for j in range(1, width):
        acc = acc + pad_ref[V7X_SUBLANES - j:V7X_SUBLANES - j + n, :] * w[width - 1 - j:width - j, :]
    pad_ref[0:V7X_SUBLANES, :] = x[n - V7X_SUBLANES:n, :]
    return acc


def _chunk_cumsum_rows(x):
    pos = lax.broadcasted_iota(jnp.int32, x.shape, 0) & (CHUNK - 1)
    d = 1
    while d < CHUNK:
        xr = pltpu.roll(x, d, 0)
        x = x + jnp.where(pos >= d, xr, 0.0)
        d *= 2
    return x


def _chunk_last_rows(x):
    n = x.shape[0]
    parts = [jnp.broadcast_to(x[c * CHUNK + CHUNK - 1:(c + 1) * CHUNK, :], (CHUNK, x.shape[1]))
             for c in range(n // CHUNK)]
    return jnp.concatenate(parts, axis=0)


def _head_rmsnorm(o, w):
    return o * lax.rsqrt(jnp.mean(o * o, axis=-1, keepdims=True) + NORM_EPS) * w


HGRN_LEVELS = (32, 16, 8, 4, 2)


def _hgrn_masks():
    n = CHUNK
    rowi = lax.broadcasted_iota(jnp.int32, (n, n), 0)
    coli = lax.broadcasted_iota(jnp.int32, (n, n), 1)
    levels = []
    for half in HGRN_LEVELS:
        blk = 2 * half
        levels.append(((rowi & -blk) == (coli & -blk)) & ((rowi & (blk - 1)) >= half) & ((coli & (blk - 1)) < half))
    row2 = lax.broadcasted_iota(jnp.int32, (n, 2 * n), 0)
    lane2 = lax.broadcasted_iota(jnp.int32, (n, 2 * n), 1)
    pair = (lane2 == row2) | ((lane2 == row2 + (n - 1)) & ((row2 & 1) == 1))
    sub = lax.broadcasted_iota(jnp.int32, (n, HGRN_DK), 0) & (V7X_SUBLANES - 1)
    return levels, pair, sub


def _level_ref_rows(g, half, sub):
    n = CHUNK
    g3 = g.reshape(n // 8, 8, HGRN_DK)
    bcast8 = lambda r: jnp.broadcast_to(g3[:, r:r + 1, :], g3.shape).reshape(n, HGRN_DK)
    if half >= 8:
        blk = 2 * half
        nb = n // blk
        return jnp.broadcast_to(g.reshape(nb, blk, HGRN_DK)[:, half:half + 1, :],
                                (nb, blk, HGRN_DK)).reshape(n, HGRN_DK)
    if half == 4:
        return bcast8(4)
    return jnp.where(sub < 4, bcast8(2), bcast8(6))


def _hgrn_block(h, w_ref, lb_ref, hw_ref, st_ref, yb_ref, *, layer, tb, between=()):
    n_slots = 4 + len(HGRN_LEVELS) + 1 + tb // CHUNK
    plan = [[] for _ in range(n_slots)]
    for i, work in enumerate(between):
        plan[i * n_slots // len(between)].append(work)

    def run_between():
        for work in plan.pop(0):
            work()

    rows = [lb_ref[i:i + 1, :] for i in range(DEPTH)]
    mx = functools.reduce(jnp.maximum, rows)
    es = [jnp.exp(r - mx) for r in rows]
    tot = functools.reduce(lambda a_, b_: a_ + b_, es)
    lb = jnp.zeros_like(tot)
    for i in range(1, layer + 1):
        lb = lb + es[i] / tot

    f_gate = (0.5 + 0.5 * lb) + (0.5 - 0.5 * lb) * jnp.tanh(0.5 * _proj(h, w_ref, HGRN_WIDTH, HGRN_WIDTH))
    f_floor = jnp.maximum(f_gate, MIN_F)
    run_between()
    g_all = _chunk_cumsum_rows(jnp.log(f_floor))
    k_all = 1.0 - f_gate
    kp_all = k_all * pltpu.roll(f_floor, tb - 1, 0)
    run_between()
    q_all = _silu(_proj(h, w_ref, 0, HGRN_WIDTH)) * (HGRN_DK ** -0.5)
    v_all = _proj(h, w_ref, 2 * HGRN_WIDTH, HGRN_WIDTH)
    run_between()
    g_last_all = _chunk_last_rows(g_all)
    qi_all = q_all * jnp.exp(g_all)
    ks_all = k_all * jnp.exp(g_last_all - g_all)
    dec_all = jnp.exp(g_last_all)
    run_between()

    n = CHUNK
    nc = tb // CHUNK
    level_masks, pair_mask, sub = _hgrn_masks()
    tiles = [(slice(c * CHUNK, (c + 1) * CHUNK), slice(hd * HGRN_DK, (hd + 1) * HGRN_DK))
             for hd in range(HGRN_HEADS) for c in range(nc)]
    qs = [q_all[t] for t in tiles]
    ks = [k_all[t] for t in tiles]
    gs = [g_all[t] for t in tiles]
    vs = [v_all[t] for t in tiles]
    scores = [None] * len(tiles)
    for half, mask in zip(HGRN_LEVELS, level_masks):
        es = [jnp.exp(-jnp.abs(g - _level_ref_rows(g, half, sub))) for g in gs]
        ss = [_dot_nt(q * e, k * e) for q, k, e in zip(qs, ks, es)]
        scores = [jnp.where(mask, s, 0.0 if a is None else a) for s, a in zip(ss, scores)]
        run_between()
    s01 = [jnp.where(pair_mask, _dot_nt(q, jnp.concatenate([k, kp_all[t]], axis=0)), 0.0)
           for q, k, t in zip(qs, ks, tiles)]
    scores = [a + (s + pltpu.roll(s, n, 1))[:, :n] for a, s in zip(scores, s01)]
    intra = [_dot(a, v) for a, v in zip(scores, vs)]
    upd = [_dot(v.T, ks_all[t]) for v, t in zip(vs, tiles)]
    run_between()

    zz = _proj(h, w_ref, 3 * HGRN_WIDTH, HGRN_WIDTH)
    hw = hw_ref[layer:layer + 1, :]
    sts = [st_ref[hd] for hd in range(HGRN_HEADS)]
    for c in range(nc):
        for hd in range(HGRN_HEADS):
            i = hd * nc + c
            rs, cs = tiles[i]
            o = intra[i] + _dot_nt(qi_all[rs, cs], sts[hd])
            sts[hd] = dec_all[c * CHUNK:c * CHUNK + 1, cs] * sts[hd] + upd[i]
            yb_ref[rs, cs] = _head_rmsnorm(o, hw) * _silu(zz[rs, cs])
        run_between()
    for hd in range(HGRN_HEADS):
        st_ref[hd] = sts[hd]
    assert not plan


def _l2norm_heads(x, heads, width):
    outs = []
    for hd in range(heads):
        xh = x[:, hd * width:(hd + 1) * width]
        outs.append(xh * lax.rsqrt(jnp.sum(xh * xh, axis=-1, keepdims=True) + L2_EPS))
    return outs


def _unit_lower_inverse_rows(p2, masks):
    upper, blk16, blk32, eye_lower, eye_upper = masks
    n = CHUNK
    zero_eye = eye_upper.astype(BF16)
    cs = [jnp.where(blk16 & ~upper, p, 0.0) + eye_upper for p in p2]
    for _ in range(4):
        cbs = [c.astype(BF16) for c in cs]
        cs = [jnp.dot(cb, jnp.concatenate([cb, zero_eye], axis=0), preferred_element_type=F32) for cb in cbs]
        yield
    for off in (blk32 & ~blk16, ~blk32):
        ys = [_dot(jnp.where(off, p, 0.0)[:, :n], c) for p, c in zip(p2, cs)]
        yield
        cs = [_dot(c + eye_lower, jnp.concatenate([c, y], axis=0)) for c, y in zip(cs, ys)]
        yield
    return cs


def _interleave(*gens):
    live = list(gens)
    while live:
        for g in list(live):
            try:
                next(g)
            except StopIteration:
                live.remove(g)


def _conv_rows(pad_ref, w, rows, width):
    base = V7X_SUBLANES + rows.start
    n = rows.stop - rows.start
    acc = pad_ref[base:base + n, :] * w[width - 1:width, :]
    for j in range(1, width):
        acc = acc + pad_ref[base - j:base - j + n, :] * w[width - 1 - j:width - j, :]
    return acc


def _gdn_kernel(x_ref, nw_ref, w_ref, cw_ref, alog_ref, dtb_ref, hw_ref,
                y_ref, st_ref, cq_ref, ck_ref, cv_ref, *, layer, tb):
    @pl.when(pl.program_id(1) == 0)
    def _():
        st_ref[...] = jnp.zeros_like(st_ref)
        for pad_ref in (cq_ref, ck_ref, cv_ref):
            pad_ref[0:V7X_SUBLANES, :] = jnp.zeros((V7X_SUBLANES, pad_ref.shape[1]), F32)

    n = CHUNK
    nc = tb // CHUNK
    rep = GDN_V_HEADS // GDN_QK_HEADS
    h = _rmsnorm_bf16(x_ref[...], nw_ref[layer:layer + 1, :])
    cw = cw_ref[...]
    cq_ref[V7X_SUBLANES:, :] = _proj(h, w_ref, WC_Q, GDN_QK_WIDTH)
    ck_ref[V7X_SUBLANES:, :] = _proj(h, w_ref, WC_K, GDN_QK_WIDTH)
    cv_ref[V7X_SUBLANES:, :] = _proj(h, w_ref, WC_V, GDN_V_WIDTH)
    sm = _proj(h, w_ref, WC_SMALL, SMALL_COLS)
    zz = _proj(h, w_ref, WC_Z, GDN_V_WIDTH)
    alog = alog_ref[layer:layer + 1, :]
    dtb = dtb_ref[layer:layer + 1, :]
    hw = hw_ref[layer:layer + 1, :]

    row2 = lax.broadcasted_iota(jnp.int32, (n, 2 * n), 0)
    lane2 = lax.broadcasted_iota(jnp.int32, (n, 2 * n), 1)
    col2 = lane2 & (n - 1)
    upper = lane2 >= n
    strict2 = row2 > col2
    causal = lax.broadcasted_iota(jnp.int32, (n, n), 0) >= lax.broadcasted_iota(jnp.int32, (n, n), 1)
    blk16 = (row2 >> 4) == (col2 >> 4)
    blk32 = (row2 >> 5) == (col2 >> 5)
    eye_lower = jnp.where((row2 == col2) & ~upper, 1.0, 0.0).astype(F32)
    eye_upper = jnp.where((row2 == col2) & upper, 1.0, 0.0).astype(F32)
    masks = (upper, blk16, blk32, eye_lower, eye_upper)
    lane = lax.broadcasted_iota(jnp.int32, (n, SMALL_COLS), 1)
    zeros_rhs = jnp.zeros((n, GDN_DV + GDN_DK), F32)

    def local_stage(c):
        rs = slice(c * CHUNK, (c + 1) * CHUNK)
        qc = _silu(_conv_rows(cq_ref, cw[:, 0:GDN_QK_WIDTH], rs, GDN_CONV_K))
        kc = _silu(_conv_rows(ck_ref, cw[:, GDN_QK_WIDTH:2 * GDN_QK_WIDTH], rs, GDN_CONV_K))
        vc = _silu(_conv_rows(cv_ref, cw[:, 2 * GDN_QK_WIDTH:], rs, GDN_CONV_K))
        qs = [x * (GDN_DK ** -0.5) for x in _l2norm_heads(qc, GDN_QK_HEADS, GDN_DK)]
        ks = _l2norm_heads(kc, GDN_QK_HEADS, GDN_DK)
        smc = sm[rs]
        beta = _sigmoid(smc)
        g = _chunk_cumsum_rows(-jnp.exp(alog) * _softplus(smc + dtb))
        comb_t = jnp.where(lane < GDN_V_HEADS, beta, g).T
        g_last = g[n - 1:n, :]
        eg_all = jnp.exp(g)
        kfac_all = pltpu.roll(beta, GDN_V_HEADS, 1) * jnp.exp(g_last - g)
        dec_all = jnp.exp(g_last)
        out = dict(p2=[], aqk=[], rhs=[], qi=[], kst=[], dec=[])
        yield
        for hq in range(GDN_QK_HEADS):
            qh, kh = qs[hq], ks[hq]
            kq2 = _dot_nt(jnp.concatenate([-kh, qh], axis=0), jnp.concatenate([kh, kh], axis=0))
            nkk2 = kq2[:n]
            qk = kq2[n:, :n]
            for r in range(rep):
                vh = hq * rep + r
                gl = GDN_V_HEADS + vh
                g_row = comb_t[gl:gl + 1, :]
                b_row = comb_t[vh:vh + 1, :]
                g_row2 = jnp.concatenate([g_row, g_row], axis=1)
                b_row2 = jnp.concatenate([b_row, b_row], axis=1)
                dec2 = jnp.exp(jnp.minimum(g[:, gl:gl + 1] - g_row2, 0.0)) * b_row2
                eg = eg_all[:, gl:gl + 1]
                out["p2"].append(jnp.where(strict2, nkk2 * dec2, 0.0))
                out["aqk"].append(jnp.where(causal, qk * dec2[:, :n], 0.0))
                out["rhs"].append(jnp.concatenate([vc[:, vh * GDN_DV:(vh + 1) * GDN_DV], kh * eg], axis=1))
                out["qi"].append(qh * eg)
                out["kst"].append(kh * kfac_all[:, gl:gl + 1])
                out["dec"].append(dec_all[:, gl:gl + 1])
            yield
        locs[c] = out

    def solve_stage(loc):
        cs = yield from _unit_lower_inverse_rows(loc["p2"], masks)
        loc["sol"] = [_dot(c_, jnp.concatenate([zeros_rhs, rhs], axis=0)) for c_, rhs in zip(cs, loc["rhs"])]

    def state_stage(c, loc, sts):
        rs = slice(c * CHUNK, (c + 1) * CHUNK)
        wq = [_dot(jnp.concatenate([sol[:, GDN_DV:], qi], axis=0), st) for sol, qi, st in zip(loc["sol"], loc["qi"], sts)]
        es = [sol[:, :GDN_DV] - r[:n] for sol, r in zip(loc["sol"], wq)]
        ae = [_dot(jnp.concatenate([aqk, kst.T], axis=0), e) for aqk, kst, e in zip(loc["aqk"], loc["kst"], es)]
        os_ = [r[n:] + a[:n] for r, a in zip(wq, ae)]
        new = [dec * st + a[n:] for dec, st, a in zip(loc["dec"], sts, ae)]
        for vh in range(GDN_V_HEADS):
            vs = slice(vh * GDN_DV, (vh + 1) * GDN_DV)
            y_ref[rs, vs] = (_head_rmsnorm(os_[vh], hw) * _silu(zz[rs, vs])).astype(BF16)
        return new

    sts = [st_ref[vh] for vh in range(GDN_V_HEADS)]
    locs = {}
    per_group = nc // GDN_GROUPS
    groups = [range(i * per_group, (i + 1) * per_group) for i in range(GDN_GROUPS)]

    def solve_group(chunks):
        merged = {key: [v for c in chunks for v in locs[c][key]] for key in ("p2", "rhs")}
        yield from solve_stage(merged)
        for i, c in enumerate(chunks):
            locs[c]["sol"] = merged["sol"][i * GDN_V_HEADS:(i + 1) * GDN_V_HEADS]

    def local_group(chunks):
        for c in chunks:
            yield from local_stage(c)

    def state_group(chunks):
        nonlocal sts
        for c in chunks:
            sts = state_stage(c, locs.pop(c), sts)
            yield

    for i in range(GDN_GROUPS + 2):
        work = []
        if 0 <= i - 1 < GDN_GROUPS:
            work.append(solve_group(groups[i - 1]))
        if i < GDN_GROUPS:
            work.append(local_group(groups[i]))
        if 0 <= i - 2 < GDN_GROUPS:
            work.append(state_group(groups[i - 2]))
        _interleave(*work)
    for vh in range(GDN_V_HEADS):
        st_ref[vh] = sts[vh]
    for pad_ref in (cq_ref, ck_ref, cv_ref):
        pad_ref[0:V7X_SUBLANES, :] = pad_ref[tb:tb + V7X_SUBLANES, :]


def _gdn(x2, norm_w, w_c, conv_c, alog_pad, dtb_pad, head_w, *, layer, bsz, seq, tb=512):
    nblk = seq // tb
    row = lambda b, s: (b * nblk + s, 0)
    block_bytes = tb * D_MODEL * 4 + tb * GDN_V_WIDTH * 2
    resident_bytes = (D_MODEL * WC_COLS * 2 + GDN_CONV_K * GDN_CONV_WIDTH * 4
                      + (GDN_V_HEADS * GDN_DK * GDN_DV + (V7X_SUBLANES + tb) * GDN_CONV_WIDTH) * 4)
    temp_bytes = 6 * tb * WC_COLS * 4
    return pl.pallas_call(
        functools.partial(_gdn_kernel, layer=layer, tb=tb),
        grid=(bsz, nblk),
        in_specs=[
            pl.BlockSpec((tb, D_MODEL), row),
            _layer_rows(D_MODEL),
            _layer_slab(layer, (D_MODEL, WC_COLS)),
            _layer_slab(layer, (GDN_CONV_K, GDN_CONV_WIDTH)),
            _layer_rows(SMALL_COLS),
            _layer_rows(SMALL_COLS),
            _layer_rows(GDN_DV),
        ],
        out_specs=pl.BlockSpec((tb, GDN_V_WIDTH), row),
        out_shape=jax.ShapeDtypeStruct((bsz * seq, GDN_V_WIDTH), BF16),
        scratch_shapes=[
            pltpu.VMEM((GDN_V_HEADS, GDN_DK, GDN_DV), F32),
            pltpu.VMEM((V7X_SUBLANES + tb, GDN_QK_WIDTH), F32),
            pltpu.VMEM((V7X_SUBLANES + tb, GDN_QK_WIDTH), F32),
            pltpu.VMEM((V7X_SUBLANES + tb, GDN_V_WIDTH), F32),
        ],
        compiler_params=pltpu.CompilerParams(
            dimension_semantics=("arbitrary", "arbitrary"),
            vmem_limit_bytes=_vmem_limit(block_bytes, resident_bytes, temp_bytes)),
        name="gdn",
    )(x2, norm_w, w_c, conv_c, alog_pad, dtb_pad, head_w)


def _hgrn_merge_kernel(x_ref, nw_ref, wb_ref, lb_ref, hw_ref, wm_ref, yc_ref, ca_ref, bg_ref,
                       wa_ref, wbo_ref, wc_ref, wo_ref, fw_ref,
                       out_ref, st_ref, pad_ref, yb_ref, *, layer, final, tb):
    @pl.when(pl.program_id(1) == 0)
    def _():
        st_ref[...] = jnp.zeros_like(st_ref)
        pad_ref[0:V7X_SUBLANES, :] = jnp.zeros((V7X_SUBLANES, CONV_WIDTH), F32)

    x = x_ref[...]
    h = _rmsnorm_bf16(x, nw_ref[layer:layer + 1, :])
    seg = lambda i: _proj(h, wm_ref, WM_A + i * CONV_WIDTH, CONV_WIDTH)
    half_bg = 0.5 * bg_ref[layer:layer + 1, :]
    tgate = lambda i: jnp.tanh(_proj(h, wm_ref, WM_G + i * D_MODEL, D_MODEL) + half_bg[:, i * D_MODEL:(i + 1) * D_MODEL])
    part = {}

    def conv_a():
        part["conv"] = _causal_conv(seg(1) * seg(2), pad_ref, ca_ref[...], CONV_K)

    def act_a():
        part["ya"] = (seg(0) * part["conv"] * _silu(seg(3))).astype(BF16)

    def branch_a():
        part["ya2"] = jnp.dot(part["ya"], wa_ref[...], preferred_element_type=F32)

    def branch_c():
        part["yc2"] = jnp.dot(yc_ref[...], wc_ref[...], preferred_element_type=F32)

    def gate_a():
        part["merged"] = tgate(0) * part["ya2"] + part["ya2"]

    def gate_c():
        part["merged"] = part["merged"] + (tgate(2) * part["yc2"] + part["yc2"])

    def gate_b():
        part["gate_b"] = tgate(1)

    _hgrn_block(h, wb_ref, lb_ref, hw_ref, st_ref, yb_ref, layer=layer, tb=tb,
                between=(conv_a, act_a, branch_a, branch_c, gate_a, gate_c, gate_b))

    yb2 = jnp.dot(yb_ref[...].astype(BF16), wbo_ref[...], preferred_element_type=F32)
    merged = part["merged"] + (part["gate_b"] * yb2 + yb2)
    out = x + jnp.dot(merged.astype(BF16), wo_ref[...], preferred_element_type=F32)
    if final:
        ms = jnp.mean(out * out, axis=-1, keepdims=True)
        out = out * lax.rsqrt(ms + NORM_EPS) * fw_ref[...]
    out_ref[...] = out


def _hgrn_merge(x2, norm_w, w_b, lower_bounds, head_w, w_m, yc, conv_a, b_gate, wa, wb, wc, wo, final_w,
                *, layer, final, bsz, seq, tb=512):
    nblk = seq // tb
    row = lambda b, s: (b * nblk + s, 0)
    block_bytes = tb * (2 * D_MODEL * 4 + GDN_V_WIDTH * 2)
    resident_bytes = ((D_MODEL * (WB_COLS + WM_COLS) + (CONV_WIDTH + HGRN_WIDTH + GDN_V_WIDTH + D_MODEL) * D_MODEL) * 2
                      + (HGRN_HEADS * HGRN_DK * HGRN_DK + (V7X_SUBLANES + tb) * CONV_WIDTH + tb * HGRN_WIDTH) * 4)
    temp_bytes = 4 * tb * (WB_COLS + WM_COLS) * 4
    return pl.pallas_call(
        functools.partial(_hgrn_merge_kernel, layer=layer, final=final, tb=tb),
        grid=(bsz, nblk),
        in_specs=[
            pl.BlockSpec((tb, D_MODEL), row),
            _layer_rows(D_MODEL),
            _layer_slab(layer, (D_MODEL, WB_COLS)),
            _resident((DEPTH, HGRN_WIDTH)),
            _layer_rows(HGRN_DK),
            _layer_slab(layer, (D_MODEL, WM_COLS)),
            pl.BlockSpec((tb, GDN_V_WIDTH), row),
            _layer_slab(layer, (CONV_K, CONV_WIDTH)),
            _layer_rows(3 * D_MODEL),
            _layer_slab(layer, (CONV_WIDTH, D_MODEL)),
            _layer_slab(layer, (HGRN_WIDTH, D_MODEL)),
            _layer_slab(layer, (GDN_V_WIDTH, D_MODEL)),
            _layer_slab(layer, (D_MODEL, D_MODEL)),
            _resident((1, D_MODEL)),
        ],
        out_specs=pl.BlockSpec((tb, D_MODEL), row),
        out_shape=jax.ShapeDtypeStruct((bsz * seq, D_MODEL), F32),
        scratch_shapes=[
            pltpu.VMEM((HGRN_HEADS, HGRN_DK, HGRN_DK), F32),
            pltpu.VMEM((V7X_SUBLANES + tb, CONV_WIDTH), F32),
            pltpu.VMEM((tb, HGRN_WIDTH), F32),
        ],
        compiler_params=pltpu.CompilerParams(
            dimension_semantics=("arbitrary", "arbitrary"),
            vmem_limit_bytes=_vmem_limit(block_bytes, resident_bytes, temp_bytes)),
        name="hgrn_merge_final" if final else "hgrn_merge",
    )(x2, norm_w, w_b, lower_bounds, head_w, w_m, yc, conv_a, b_gate, wa, wb, wc, wo, final_w)


PREP_ROWS = 128
PREP_COLS = 1024
PREP_NB, PREP_NC, PREP_NM = WB_COLS // PREP_COLS, -(-WC_COLS // PREP_COLS), WM_COLS // PREP_COLS


def _prep_sources():
    span = lambda start, width: [start + i * PREP_COLS for i in range(width // PREP_COLS)]
    return (span(SRC_B, WB_COLS)
            + span(SRC_CQKV, GDN_CONV_WIDTH) + span(SRC_CZ, GDN_V_WIDTH) + [SRC_SMALL]
            + span(SRC_A, 4 * CONV_WIDTH) + span(SRC_G, 3 * D_MODEL))


def _prep_in_kernel(src_ref, wt_ref, pb_ref, pc_ref, pm_ref):
    del src_ref
    j = pl.program_id(1)
    small = PREP_NB + PREP_NC - 1
    gates = PREP_NB + PREP_NC + 4 * CONV_WIDTH // PREP_COLS
    blk = wt_ref[0].T

    @pl.when(j < PREP_NB)
    def _():
        pb_ref[...] = blk.astype(BF16)

    @pl.when(jnp.logical_and(j >= PREP_NB, j < small))
    def _():
        pc_ref[...] = blk.astype(BF16)

    @pl.when(j == small)
    def _():
        lane = lax.broadcasted_iota(jnp.int32, (D_MODEL, PREP_COLS), 1)
        pc_ref[...] = jnp.where(lane < 2 * GDN_V_HEADS, blk, 0.0).astype(BF16)

    @pl.when(jnp.logical_and(j > small, j < gates))
    def _():
        pm_ref[...] = blk.astype(BF16)

    @pl.when(j >= gates)
    def _():
        pm_ref[...] = (0.5 * blk).astype(BF16)


def _prep_in_weights(w_in):
    w_t = jnp.swapaxes(w_in, 1, 2)
    src = jnp.asarray([c // V7X_SUBLANES for c in _prep_sources()], jnp.int32)
    out_spec = lambda first, count: pl.BlockSpec(
        (None, D_MODEL, PREP_COLS), lambda l, j, src_ref: (l, 0, jnp.clip(j - first, 0, count - 1)))
    out = lambda cols: jax.ShapeDtypeStruct((DEPTH, D_MODEL, cols), BF16)
    block_bytes = PREP_COLS * D_MODEL * (4 + 3 * 2)
    return pl.pallas_call(
        _prep_in_kernel,
        grid_spec=pltpu.PrefetchScalarGridSpec(
            num_scalar_prefetch=1,
            grid=(DEPTH, PREP_NB + PREP_NC + PREP_NM),
            in_specs=[pl.BlockSpec((pl.Element(1), pl.Element(PREP_COLS), pl.Element(D_MODEL)),
                                   lambda l, j, src_ref: (l, src_ref[j] * V7X_SUBLANES, 0))],
            out_specs=[out_spec(0, PREP_NB), out_spec(PREP_NB, PREP_NC), out_spec(PREP_NB + PREP_NC, PREP_NM)],
        ),
        out_shape=[out(WB_COLS), out(WC_COLS), out(WM_COLS)],
        compiler_params=pltpu.CompilerParams(
            dimension_semantics=("arbitrary", "arbitrary"),
            vmem_limit_bytes=_vmem_limit(block_bytes, 0, 4 * PREP_COLS * D_MODEL * 4)),
        name="prep_in_weights",
    )(src, w_t)


def _prep_out_kernel(wa_ref, wb_ref, wc_ref, wo_ref, oa_ref, ob_ref, oc_ref, oo_ref):
    oa_ref[...] = wa_ref[...].astype(BF16)
    ob_ref[...] = wb_ref[...].astype(BF16)
    oc_ref[...] = wc_ref[...].astype(BF16)
    oo_ref[...] = (0.5 * wo_ref[...]).astype(BF16)


def _prep_out_weights(w_out_a, w_out_b, w_out_c, w_o):
    steps = D_MODEL // PREP_ROWS
    rows = (CONV_WIDTH, HGRN_WIDTH, GDN_V_WIDTH, D_MODEL)
    slab = lambda r: pl.BlockSpec((None, r // steps, D_MODEL), lambda l, i: (l, i, 0))
    return pl.pallas_call(
        _prep_out_kernel,
        grid=(DEPTH, steps),
        in_specs=[slab(r) for r in rows],
        out_specs=[slab(r) for r in rows],
        out_shape=[jax.ShapeDtypeStruct((DEPTH, r, D_MODEL), BF16) for r in rows],
        compiler_params=pltpu.CompilerParams(
            dimension_semantics=("arbitrary", "arbitrary"),
            vmem_limit_bytes=_vmem_limit(sum(rows) // steps * D_MODEL * 6, 0, 0)),
        name="prep_out_weights",
    )(w_out_a, w_out_b, w_out_c, w_o)


def kernel(x, norm_w, w_in, b_gate, conv_a, conv_c, a_log, dt_bias, lower_bounds, hgrn_norm_w, gdn_norm_w,
           w_out_a, w_out_b, w_out_c, w_o, final_norm_w):
    bsz, seq, _ = x.shape
    x2 = x.reshape(bsz * seq, D_MODEL)
    w_b, w_c, w_m = _prep_in_weights(w_in)
    wa, wb, wc, wo = _prep_out_weights(w_out_a, w_out_b, w_out_c, w_o)
    pad8 = lambda a: jnp.pad(a.astype(F32), ((0, 0), (GDN_V_HEADS, SMALL_COLS - 2 * GDN_V_HEADS)))
    alog_pad, dtb_pad = pad8(a_log), pad8(dt_bias)
    final_w = final_norm_w[None, :]
    for l in range(DEPTH):
        yc = _gdn(x2, norm_w, w_c, conv_c, alog_pad, dtb_pad, gdn_norm_w, layer=l, bsz=bsz, seq=seq)
        x2 = _hgrn_merge(x2, norm_w, w_b, lower_bounds, hgrn_norm_w, w_m, yc, conv_a, b_gate, wa, wb, wc, wo,
                         final_w, layer=l, final=(l == DEPTH - 1), bsz=bsz, seq=seq)
    return x2.reshape(bsz, seq, D_MODEL)
```

```python
import functools

import jax
import jax.numpy as jnp
from jax import lax
from jax.experimental import pallas as pl
from jax.experimental.pallas import tpu as pltpu

F32 = jnp.float32
BF16 = jnp.bfloat16

D_MODEL = 1024
DEPTH = 2
CHUNK = 64
NORM_EPS = 1e-6
L2_EPS = 1e-6
MIN_F = 1e-30
CONV_WIDTH = 512
CONV_K = 3
HGRN_HEADS = 4
HGRN_DK = 128
HGRN_WIDTH = HGRN_HEADS * HGRN_DK
GDN_QK_HEADS = 4
GDN_V_HEADS = 8
GDN_DK = 128
GDN_DV = 128
GDN_CONV_K = 4
GDN_QK_WIDTH = GDN_QK_HEADS * GDN_DK
GDN_V_WIDTH = GDN_V_HEADS * GDN_DV
GDN_CONV_WIDTH = 2 * GDN_QK_WIDTH + GDN_V_WIDTH
GDN_GROUPS = 2

SRC_A = 0
SRC_B = SRC_A + 4 * CONV_WIDTH
SRC_CQKV = SRC_B + 4 * HGRN_WIDTH
SRC_SMALL = SRC_CQKV + GDN_CONV_WIDTH
SRC_CZ = SRC_SMALL + 2 * GDN_V_HEADS
SRC_G = SRC_CZ + GDN_V_WIDTH
SMALL_COLS = 128

WB_COLS = 4 * HGRN_WIDTH
WC_Q, WC_K, WC_V, WC_Z, WC_SMALL = 0, 512, 1024, 2048, 3072
WC_COLS = WC_SMALL + SMALL_COLS
WM_A, WM_G = 0, 4 * CONV_WIDTH
WM_COLS = WM_G + 3 * D_MODEL

V7X_SUBLANES = 8
V7X_VMEM_BYTES = 64 * 1024 * 1024


def _vmem_limit(block_bytes, resident_bytes, temp_bytes):
    need = 2 * block_bytes + resident_bytes + temp_bytes
    return int(min(V7X_VMEM_BYTES - 8 * 1024 * 1024, need + need // 4 + 4 * 1024 * 1024))


def _resident(shape):
    return pl.BlockSpec(shape, lambda b, s: (0,) * len(shape), pipeline_mode=pl.Buffered(1))


def _layer_rows(width):
    return _resident((DEPTH, width))


def _layer_slab(layer, shape):
    return pl.BlockSpec((None,) + tuple(shape), lambda b, s: (layer,) + (0,) * len(shape),
                        pipeline_mode=pl.Buffered(1))


def _sigmoid(x):
    return 0.5 * jnp.tanh(0.5 * x) + 0.5


def _silu(x):
    half = 0.5 * x
    return half * jnp.tanh(half) + half


def _softplus(x):
    return jnp.maximum(x, 0.0) + jnp.log(1.0 + jnp.exp(-jnp.abs(x)))


def _dot(a, b):
    return jnp.dot(a.astype(BF16), b.astype(BF16), preferred_element_type=F32)


def _dot_nt(a, b):
    return lax.dot_general(a.astype(BF16), b.astype(BF16), (((1,), (1,)), ((), ())),
                           preferred_element_type=F32)


def _rmsnorm_bf16(x, w):
    ms = jnp.mean(x * x, axis=-1, keepdims=True)
    return (x * lax.rsqrt(ms + NORM_EPS) * w).astype(BF16)


def _proj(h, w_ref, start, width):
    return jnp.dot(h, w_ref[:, start:start + width], preferred_element_type=F32)


def _causal_conv(x, pad_ref, w, width):
    n = x.shape[0]
    pad_ref[V7X_SUBLANES:V7X_SUBLANES + n, :] = x
    acc = x * w[width - 1:width, :]
    for j in range(1, width):
        acc = acc + pad_ref[V7X_SUBLANES - j:V7X_SUBLANES - j + n, :] * w[width - 1 - j:width - j, :]
    pad_ref[0:V7X_SUBLANES, :] = x[n - V7X_SUBLANES:n, :]
    return acc


def _chunk_cumsum_rows(x):
    pos = lax.broadcasted_iota(jnp.int32, x.shape, 0) & (CHUNK - 1)
    d = 1
    while d < CHUNK:
        xr = pltpu.roll(x, d, 0)
        x = x + jnp.where(pos >= d, xr, 0.0)
        d *= 2
    return x


def _chunk_last_rows(x):
    n = x.shape[0]
    parts = [jnp.broadcast_to(x[c * CHUNK + CHUNK - 1:(c + 1) * CHUNK, :], (CHUNK, x.shape[1]))
             for c in range(n // CHUNK)]
    return jnp.concatenate(parts, axis=0)


def _head_rmsnorm(o, w):
    return o * lax.rsqrt(jnp.mean(o * o, axis=-1, keepdims=True) + NORM_EPS) * w


HGRN_LEVELS = (32, 16, 8, 4, 2)


def _hgrn_masks():
    n = CHUNK
    rowi = lax.broadcasted_iota(jnp.int32, (n, n), 0)
    coli = lax.broadcasted_iota(jnp.int32, (n, n), 1)
    levels = []
    for half in HGRN_LEVELS:
        blk = 2 * half
        levels.append(((rowi & -blk) == (coli & -blk)) & ((rowi & (blk - 1)) >= half) & ((coli & (blk - 1)) < half))
    row2 = lax.broadcasted_iota(jnp.int32, (n, 2 * n), 0)
    lane2 = lax.broadcasted_iota(jnp.int32, (n, 2 * n), 1)
    pair = (lane2 == row2) | ((lane2 == row2 + (n - 1)) & ((row2 & 1) == 1))
    sub = lax.broadcasted_iota(jnp.int32, (n, HGRN_DK), 0) & (V7X_SUBLANES - 1)
    return levels, pair, sub


def _level_ref_rows(g, half, sub):
    n = CHUNK
    g3 = g.reshape(n // 8, 8, HGRN_DK)
    bcast8 = lambda r: jnp.broadcast_to(g3[:, r:r + 1, :], g3.shape).reshape(n, HGRN_DK)
    if half >= 8:
        blk = 2 * half
        nb = n // blk
        return jnp.broadcast_to(g.reshape(nb, blk, HGRN_DK)[:, half:half + 1, :],
                                (nb, blk, HGRN_DK)).reshape(n, HGRN_DK)
    if half == 4:
        return bcast8(4)
    return jnp.where(sub < 4, bcast8(2), bcast8(6))


def _hgrn_block(h, w_ref, lb_ref, hw_ref, st_ref, yb_ref, *, layer, tb, between=()):
    n_slots = 4 + len(HGRN_LEVELS) + 1 + tb // CHUNK
    plan = [[] for _ in range(n_slots)]
    for i, work in enumerate(between):
        plan[i * n_slots // len(between)].append(work)

    def run_between():
        for work in plan.pop(0):
            work()

    rows = [lb_ref[i:i + 1, :] for i in range(DEPTH)]
    mx = functools.reduce(jnp.maximum, rows)
    es = [jnp.exp(r - mx) for r in rows]
    tot = functools.reduce(lambda a_, b_: a_ + b_, es)
    lb = jnp.zeros_like(tot)
    for i in range(1, layer + 1):
        lb = lb + es[i] / tot

    f_gate = (0.5 + 0.5 * lb) + (0.5 - 0.5 * lb) * jnp.tanh(0.5 * _proj(h, w_ref, HGRN_WIDTH, HGRN_WIDTH))
    f_floor = jnp.maximum(f_gate, MIN_F)
    run_between()
    g_all = _chunk_cumsum_rows(jnp.log(f_floor))
    k_all = 1.0 - f_gate
    kp_all = k_all * pltpu.roll(f_floor, tb - 1, 0)
    run_between()
    q_all = _silu(_proj(h, w_ref, 0, HGRN_WIDTH)) * (HGRN_DK ** -0.5)
    v_all = _proj(h, w_ref, 2 * HGRN_WIDTH, HGRN_WIDTH)
    run_between()
    g_last_all = _chunk_last_rows(g_all)
    qi_all = q_all * jnp.exp(g_all)
    ks_all = k_all * jnp.exp(g_last_all - g_all)
    dec_all = jnp.exp(g_last_all)
    run_between()

    n = CHUNK
    nc = tb // CHUNK
    level_masks, pair_mask, sub = _hgrn_masks()
    tiles = [(slice(c * CHUNK, (c + 1) * CHUNK), slice(hd * HGRN_DK, (hd + 1) * HGRN_DK))
             for hd in range(HGRN_HEADS) for c in range(nc)]
    qs = [q_all[t] for t in tiles]
    ks = [k_all[t] for t in tiles]
    gs = [g_all[t] for t in tiles]
    vs = [v_all[t] for t in tiles]
    scores = [None] * len(tiles)
    for half, mask in zip(HGRN_LEVELS, level_masks):
        es = [jnp.exp(-jnp.abs(g - _level_ref_rows(g, half, sub))) for g in gs]
        ss = [_dot_nt(q * e, k * e) for q, k, e in zip(qs, ks, es)]
        scores = [jnp.where(mask, s, 0.0 if a is None else a) for s, a in zip(ss, scores)]
        run_between()
    s01 = [jnp.where(pair_mask, _dot_nt(q, jnp.concatenate([k, kp_all[t]], axis=0)), 0.0)
           for q, k, t in zip(qs, ks, tiles)]
    scores = [a + (s + pltpu.roll(s, n, 1))[:, :n] for a, s in zip(scores, s01)]
    intra = [_dot(a, v) for a, v in zip(scores, vs)]
    upd = [_dot(v.T, ks_all[t]) for v, t in zip(vs, tiles)]
    run_between()

    zz = _proj(h, w_ref, 3 * HGRN_WIDTH, HGRN_WIDTH)
    hw = hw_ref[layer:layer + 1, :]
    sts = [st_ref[hd] for hd in range(HGRN_HEADS)]
    for c in range(nc):
        for hd in range(HGRN_HEADS):
            i = hd * nc + c
            rs, cs = tiles[i]
            o = intra[i] + _dot_nt(qi_all[rs, cs], sts[hd])
            sts[hd] = dec_all[c * CHUNK:c * CHUNK + 1, cs] * sts[hd] + upd[i]
            yb_ref[rs, cs] = (_head_rmsnorm(o, hw) * _silu(zz[rs, cs])).astype(yb_ref.dtype)
        run_between()
    for hd in range(HGRN_HEADS):
        st_ref[hd] = sts[hd]
    assert not plan


def _l2norm_heads(x, heads, width):
    outs = []
    for hd in range(heads):
        xh = x[:, hd * width:(hd + 1) * width]
        outs.append(xh * lax.rsqrt(jnp.sum(xh * xh, axis=-1, keepdims=True) + L2_EPS))
    return outs


def _unit_lower_inverse_rows(p2, masks):
    upper, blk16, blk32, eye_lower, eye_upper = masks
    n = CHUNK
    zero_eye = eye_upper.astype(BF16)
    cs = [jnp.where(blk16 & ~upper, p, 0.0) + eye_upper for p in p2]
    for _ in range(4):
        cbs = [c.astype(BF16) for c in cs]
        cs = [jnp.dot(cb, jnp.concatenate([cb, zero_eye], axis=0), preferred_element_type=F32) for cb in cbs]
        yield
    for off in (blk32 & ~blk16, ~blk32):
        ys = [_dot(jnp.where(off, p, 0.0)[:, :n], c) for p, c in zip(p2, cs)]
        yield
        cs = [_dot(c + eye_lower, jnp.concatenate([c, y], axis=0)) for c, y in zip(cs, ys)]
        yield
    return cs


def _interleave(*gens):
    live = list(gens)
    while live:
        for g in list(live):
            try:
                next(g)
            except StopIteration:
                live.remove(g)


def _conv_rows(pad_ref, w, rows, width):
    base = V7X_SUBLANES + rows.start
    n = rows.stop - rows.start
    acc = pad_ref[base:base + n, :] * w[width - 1:width, :]
    for j in range(1, width):
        acc = acc + pad_ref[base - j:base - j + n, :] * w[width - 1 - j:width - j, :]
    return acc


def _gdn_kernel(x_ref, nw_ref, w_ref, cw_ref, alog_ref, dtb_ref, hw_ref,
                y_ref, st_ref, cq_ref, ck_ref, cv_ref, *, layer, tb):
    @pl.when(pl.program_id(1) == 0)
    def _():
        st_ref[...] = jnp.zeros_like(st_ref)
        for pad_ref in (cq_ref, ck_ref, cv_ref):
            pad_ref[0:V7X_SUBLANES, :] = jnp.zeros((V7X_SUBLANES, pad_ref.shape[1]), F32)

    n = CHUNK
    nc = tb // CHUNK
    rep = GDN_V_HEADS // GDN_QK_HEADS
    h = _rmsnorm_bf16(x_ref[...], nw_ref[layer:layer + 1, :])
    cw = cw_ref[...]
    cq_ref[V7X_SUBLANES:, :] = _proj(h, w_ref, WC_Q, GDN_QK_WIDTH)
    ck_ref[V7X_SUBLANES:, :] = _proj(h, w_ref, WC_K, GDN_QK_WIDTH)
    cv_ref[V7X_SUBLANES:, :] = _proj(h, w_ref, WC_V, GDN_V_WIDTH)
    sm = _proj(h, w_ref, WC_SMALL, SMALL_COLS)
    zz = _proj(h, w_ref, WC_Z, GDN_V_WIDTH)
    alog = alog_ref[layer:layer + 1, :]
    dtb = dtb_ref[layer:layer + 1, :]
    hw = hw_ref[layer:layer + 1, :]

    row2 = lax.broadcasted_iota(jnp.int32, (n, 2 * n), 0)
    lane2 = lax.broadcasted_iota(jnp.int32, (n, 2 * n), 1)
    col2 = lane2 & (n - 1)
    upper = lane2 >= n
    strict2 = row2 > col2
    causal = lax.broadcasted_iota(jnp.int32, (n, n), 0) >= lax.broadcasted_iota(jnp.int32, (n, n), 1)
    blk16 = (row2 >> 4) == (col2 >> 4)
    blk32 = (row2 >> 5) == (col2 >> 5)
    eye_lower = jnp.where((row2 == col2) & ~upper, 1.0, 0.0).astype(F32)
    eye_upper = jnp.where((row2 == col2) & upper, 1.0, 0.0).astype(F32)
    masks = (upper, blk16, blk32, eye_lower, eye_upper)
    lane = lax.broadcasted_iota(jnp.int32, (n, SMALL_COLS), 1)
    zeros_rhs = jnp.zeros((n, GDN_DV + GDN_DK), F32)

    def local_stage(c):
        rs = slice(c * CHUNK, (c + 1) * CHUNK)
        qc = _silu(_conv_rows(cq_ref, cw[:, 0:GDN_QK_WIDTH], rs, GDN_CONV_K))
        kc = _silu(_conv_rows(ck_ref, cw[:, GDN_QK_WIDTH:2 * GDN_QK_WIDTH], rs, GDN_CONV_K))
        vc = _silu(_conv_rows(cv_ref, cw[:, 2 * GDN_QK_WIDTH:], rs, GDN_CONV_K))
        qs = [x * (GDN_DK ** -0.5) for x in _l2norm_heads(qc, GDN_QK_HEADS, GDN_DK)]
        ks = _l2norm_heads(kc, GDN_QK_HEADS, GDN_DK)
        smc = sm[rs]
        beta = _sigmoid(smc)
        g = _chunk_cumsum_rows(-jnp.exp(alog) * _softplus(smc + dtb))
        comb_t = jnp.where(lane < GDN_V_HEADS, beta, g).T
        g_last = g[n - 1:n, :]
        eg_all = jnp.exp(g)
        kfac_all = pltpu.roll(beta, GDN_V_HEADS, 1) * jnp.exp(g_last - g)
        dec_all = jnp.exp(g_last)
        out = dict(p2=[], aqk=[], rhs=[], qi=[], kst=[], dec=[])
        yield
        for hq in range(GDN_QK_HEADS):
            qh, kh = qs[hq], ks[hq]
            kq2 = _dot_nt(jnp.concatenate([-kh, qh], axis=0), jnp.concatenate([kh, kh], axis=0))
            nkk2 = kq2[:n]
            qk = kq2[n:, :n]
            for r in range(rep):
                vh = hq * rep + r
                gl = GDN_V_HEADS + vh
                g_row = comb_t[gl:gl + 1, :]
                b_row = comb_t[vh:vh + 1, :]
                g_row2 = jnp.concatenate([g_row, g_row], axis=1)
                b_row2 = jnp.concatenate([b_row, b_row], axis=1)
                dec2 = jnp.exp(jnp.minimum(g[:, gl:gl + 1] - g_row2, 0.0)) * b_row2
                eg = eg_all[:, gl:gl + 1]
                out["p2"].append(jnp.where(strict2, nkk2 * dec2, 0.0))
                out["aqk"].append(jnp.where(causal, qk * dec2[:, :n], 0.0))
                out["rhs"].append(jnp.concatenate([vc[:, vh * GDN_DV:(vh + 1) * GDN_DV], kh * eg], axis=1))
                out["qi"].append(qh * eg)
                out["kst"].append(kh * kfac_all[:, gl:gl + 1])
                out["dec"].append(dec_all[:, gl:gl + 1])
            yield
        locs[c] = out

    def solve_stage(loc):
        cs = yield from _unit_lower_inverse_rows(loc["p2"], masks)
        loc["sol"] = [_dot(c_, jnp.concatenate([zeros_rhs, rhs], axis=0)) for c_, rhs in zip(cs, loc["rhs"])]

    def state_stage(c, loc, sts):
        rs = slice(c * CHUNK, (c + 1) * CHUNK)
        wq = [_dot(jnp.concatenate([sol[:, GDN_DV:], qi], axis=0), st) for sol, qi, st in zip(loc["sol"], loc["qi"], sts)]
        es = [sol[:, :GDN_DV] - r[:n] for sol, r in zip(loc["sol"], wq)]
        ae = [_dot(jnp.concatenate([aqk, kst.T], axis=0), e) for aqk, kst, e in zip(loc["aqk"], loc["kst"], es)]
        os_ = [r[n:] + a[:n] for r, a in zip(wq, ae)]
        new = [dec * st + a[n:] for dec, st, a in zip(loc["dec"], sts, ae)]
        for vh in range(GDN_V_HEADS):
            vs = slice(vh * GDN_DV, (vh + 1) * GDN_DV)
            y_ref[rs, vs] = (_head_rmsnorm(os_[vh], hw) * _silu(zz[rs, vs])).astype(BF16)
        return new

    sts = [st_ref[vh] for vh in range(GDN_V_HEADS)]
    locs = {}
    per_group = nc // GDN_GROUPS
    groups = [range(i * per_group, (i + 1) * per_group) for i in range(GDN_GROUPS)]

    def solve_group(chunks):
        merged = {key: [v for c in chunks for v in locs[c][key]] for key in ("p2", "rhs")}
        yield from solve_stage(merged)
        for i, c in enumerate(chunks):
            locs[c]["sol"] = merged["sol"][i * GDN_V_HEADS:(i + 1) * GDN_V_HEADS]

    def local_group(chunks):
        for c in chunks:
            yield from local_stage(c)

    def state_group(chunks):
        nonlocal sts
        for c in chunks:
            sts = state_stage(c, locs.pop(c), sts)
            yield

    for i in range(GDN_GROUPS + 2):
        work = []
        if 0 <= i - 1 < GDN_GROUPS:
            work.append(solve_group(groups[i - 1]))
        if i < GDN_GROUPS:
            work.append(local_group(groups[i]))
        if 0 <= i - 2 < GDN_GROUPS:
            work.append(state_group(groups[i - 2]))
        _interleave(*work)
    for vh in range(GDN_V_HEADS):
        st_ref[vh] = sts[vh]
    for pad_ref in (cq_ref, ck_ref, cv_ref):
        pad_ref[0:V7X_SUBLANES, :] = pad_ref[tb:tb + V7X_SUBLANES, :]


def _gdn(x2, norm_w, w_c, conv_c, alog_pad, dtb_pad, head_w, *, layer, bsz, seq, tb=512):
    nblk = seq // tb
    row = lambda b, s: (b * nblk + s, 0)
    block_bytes = tb * D_MODEL * 4 + tb * GDN_V_WIDTH * 2
    resident_bytes = (D_MODEL * WC_COLS * 2 + GDN_CONV_K * GDN_CONV_WIDTH * 4
                      + (GDN_V_HEADS * GDN_DK * GDN_DV + (V7X_SUBLANES + tb) * GDN_CONV_WIDTH) * 4)
    temp_bytes = 6 * tb * WC_COLS * 4
    return pl.pallas_call(
        functools.partial(_gdn_kernel, layer=layer, tb=tb),
        grid=(bsz, nblk),
        in_specs=[
            pl.BlockSpec((tb, D_MODEL), row),
            _layer_rows(D_MODEL),
            _layer_slab(layer, (D_MODEL, WC_COLS)),
            _layer_slab(layer, (GDN_CONV_K, GDN_CONV_WIDTH)),
            _layer_rows(SMALL_COLS),
            _layer_rows(SMALL_COLS),
            _layer_rows(GDN_DV),
        ],
        out_specs=pl.BlockSpec((tb, GDN_V_WIDTH), row),
        out_shape=jax.ShapeDtypeStruct((bsz * seq, GDN_V_WIDTH), BF16),
        scratch_shapes=[
            pltpu.VMEM((GDN_V_HEADS, GDN_DK, GDN_DV), F32),
            pltpu.VMEM((V7X_SUBLANES + tb, GDN_QK_WIDTH), F32),
            pltpu.VMEM((V7X_SUBLANES + tb, GDN_QK_WIDTH), F32),
            pltpu.VMEM((V7X_SUBLANES + tb, GDN_V_WIDTH), F32),
        ],
        compiler_params=pltpu.CompilerParams(
            dimension_semantics=("arbitrary", "arbitrary"),
            vmem_limit_bytes=_vmem_limit(block_bytes, resident_bytes, temp_bytes)),
        name="gdn",
    )(x2, norm_w, w_c, conv_c, alog_pad, dtb_pad, head_w)


def _hgrn_merge_kernel(x_ref, nw_ref, wb_ref, lb_ref, hw_ref, wm_ref, yc_ref, ca_ref, bg_ref,
                       wa_ref, wbo_ref, wc_ref, wo_ref, fw_ref,
                       out_ref, st_ref, pad_ref, yb_ref, *, layer, final, tb):
    @pl.when(pl.program_id(1) == 0)
    def _():
        st_ref[...] = jnp.zeros_like(st_ref)
        pad_ref[0:V7X_SUBLANES, :] = jnp.zeros((V7X_SUBLANES, CONV_WIDTH), F32)

    x = x_ref[...]
    h = _rmsnorm_bf16(x, nw_ref[layer:layer + 1, :])
    seg = lambda i: _proj(h, wm_ref, WM_A + i * CONV_WIDTH, CONV_WIDTH)
    half_bg = 0.5 * bg_ref[layer:layer + 1, :]
    tgate = lambda i: jnp.tanh(_proj(h, wm_ref, WM_G + i * D_MODEL, D_MODEL) + half_bg[:, i * D_MODEL:(i + 1) * D_MODEL])
    part = {}

    def conv_a():
        part["conv"] = _causal_conv(seg(1) * seg(2), pad_ref, ca_ref[...], CONV_K)

    def act_a():
        part["ya"] = (seg(0) * part["conv"] * _silu(seg(3))).astype(BF16)

    def branch_a():
        part["ya2"] = jnp.dot(part["ya"], wa_ref[...], preferred_element_type=F32)

    def branch_c():
        part["yc2"] = jnp.dot(yc_ref[...], wc_ref[...], preferred_element_type=F32)

    def gate_a():
        part["merged"] = tgate(0) * part["ya2"] + part["ya2"]

    def gate_c():
        part["merged"] = part["merged"] + (tgate(2) * part["yc2"] + part["yc2"])

    def gate_b():
        part["gate_b"] = tgate(1)

    _hgrn_block(h, wb_ref, lb_ref, hw_ref, st_ref, yb_ref, layer=layer, tb=tb,
                between=(conv_a, act_a, branch_a, branch_c, gate_a, gate_c, gate_b))

    yb2 = jnp.dot(yb_ref[...], wbo_ref[...], preferred_element_type=F32)
    merged = part["merged"] + (part["gate_b"] * yb2 + yb2)
    out = x + jnp.dot(merged.astype(BF16), wo_ref[...], preferred_element_type=F32)
    if final:
        ms = jnp.mean(out * out, axis=-1, keepdims=True)
        out = out * lax.rsqrt(ms + NORM_EPS) * fw_ref[...]
    out_ref[...] = out


def _hgrn_merge(x2, norm_w, w_b, lower_bounds, head_w, w_m, yc, conv_a, b_gate, wa, wb, wc, wo, final_w,
                *, layer, final, bsz, seq, tb=512):
    nblk = seq // tb
    row = lambda b, s: (b * nblk + s, 0)
    block_bytes = tb * (2 * D_MODEL * 4 + GDN_V_WIDTH * 2)
    resident_bytes = ((D_MODEL * (WB_COLS + WM_COLS) + (CONV_WIDTH + HGRN_WIDTH + GDN_V_WIDTH + D_MODEL) * D_MODEL) * 2
                      + (HGRN_HEADS * HGRN_DK * HGRN_DK + (V7X_SUBLANES + tb) * CONV_WIDTH + tb * HGRN_WIDTH) * 4)
    temp_bytes = 4 * tb * (WB_COLS + WM_COLS) * 4
    return pl.pallas_call(
        functools.partial(_hgrn_merge_kernel, layer=layer, final=final, tb=tb),
        grid=(bsz, nblk),
        in_specs=[
            pl.BlockSpec((tb, D_MODEL), row),
            _layer_rows(D_MODEL),
            _layer_slab(layer, (D_MODEL, WB_COLS)),
            _resident((DEPTH, HGRN_WIDTH)),
            _layer_rows(HGRN_DK),
            _layer_slab(layer, (D_MODEL, WM_COLS)),
            pl.BlockSpec((tb, GDN_V_WIDTH), row),
            _layer_slab(layer, (CONV_K, CONV_WIDTH)),
            _layer_rows(3 * D_MODEL),
            _layer_slab(layer, (CONV_WIDTH, D_MODEL)),
            _layer_slab(layer, (HGRN_WIDTH, D_MODEL)),
            _layer_slab(layer, (GDN_V_WIDTH, D_MODEL)),
            _layer_slab(layer, (D_MODEL, D_MODEL)),
            _resident((1, D_MODEL)),
        ],
        out_specs=pl.BlockSpec((tb, D_MODEL), row),
        out_shape=jax.ShapeDtypeStruct((bsz * seq, D_MODEL), F32),
        scratch_shapes=[
            pltpu.VMEM((HGRN_HEADS, HGRN_DK, HGRN_DK), F32),
            pltpu.VMEM((V7X_SUBLANES + tb, CONV_WIDTH), F32),
            pltpu.VMEM((tb, HGRN_WIDTH), BF16),
        ],
        compiler_params=pltpu.CompilerParams(
            dimension_semantics=("arbitrary", "arbitrary"),
            vmem_limit_bytes=_vmem_limit(block_bytes, resident_bytes, temp_bytes)),
        name="hgrn_merge_final" if final else "hgrn_merge",
    )(x2, norm_w, w_b, lower_bounds, head_w, w_m, yc, conv_a, b_gate, wa, wb, wc, wo, final_w)


PREP_ROWS = 128
PREP_COLS = 1024
PREP_NB, PREP_NC, PREP_NM = WB_COLS // PREP_COLS, -(-WC_COLS // PREP_COLS), WM_COLS // PREP_COLS


def _prep_sources():
    span = lambda start, width: [start + i * PREP_COLS for i in range(width // PREP_COLS)]
    return (span(SRC_B, WB_COLS)
            + span(SRC_CQKV, GDN_CONV_WIDTH) + span(SRC_CZ, GDN_V_WIDTH) + [SRC_SMALL]
            + span(SRC_A, 4 * CONV_WIDTH) + span(SRC_G, 3 * D_MODEL))


def _prep_in_kernel(src_ref, wt_ref, pb_ref, pc_ref, pm_ref):
    del src_ref
    j = pl.program_id(1)
    small = PREP_NB + PREP_NC - 1
    gates = PREP_NB + PREP_NC + 4 * CONV_WIDTH // PREP_COLS
    blk = wt_ref[0].T

    @pl.when(j < PREP_NB)
    def _():
        pb_ref[...] = blk.astype(BF16)

    @pl.when(jnp.logical_and(j >= PREP_NB, j < small))
    def _():
        pc_ref[...] = blk.astype(BF16)

    @pl.when(j == small)
    def _():
        lane = lax.broadcasted_iota(jnp.int32, (D_MODEL, PREP_COLS), 1)
        pc_ref[...] = jnp.where(lane < 2 * GDN_V_HEADS, blk, 0.0).astype(BF16)

    @pl.when(jnp.logical_and(j > small, j < gates))
    def _():
        pm_ref[...] = blk.astype(BF16)

    @pl.when(j >= gates)
    def _():
        pm_ref[...] = (0.5 * blk).astype(BF16)


def _prep_in_weights(w_in):
    w_t = jnp.swapaxes(w_in, 1, 2)
    src = jnp.asarray([c // V7X_SUBLANES for c in _prep_sources()], jnp.int32)
    out_spec = lambda first, count: pl.BlockSpec(
        (None, D_MODEL, PREP_COLS), lambda l, j, src_ref: (l, 0, jnp.clip(j - first, 0, count - 1)))
    out = lambda cols: jax.ShapeDtypeStruct((DEPTH, D_MODEL, cols), BF16)
    block_bytes = PREP_COLS * D_MODEL * (4 + 3 * 2)
    return pl.pallas_call(
        _prep_in_kernel,
        grid_spec=pltpu.PrefetchScalarGridSpec(
            num_scalar_prefetch=1,
            grid=(DEPTH, PREP_NB + PREP_NC + PREP_NM),
            in_specs=[pl.BlockSpec((pl.Element(1), pl.Element(PREP_COLS), pl.Element(D_MODEL)),
                                   lambda l, j, src_ref: (l, src_ref[j] * V7X_SUBLANES, 0))],
            out_specs=[out_spec(0, PREP_NB), out_spec(PREP_NB, PREP_NC), out_spec(PREP_NB + PREP_NC, PREP_NM)],
        ),
        out_shape=[out(WB_COLS), out(WC_COLS), out(WM_COLS)],
        compiler_params=pltpu.CompilerParams(
            dimension_semantics=("arbitrary", "arbitrary"),
            vmem_limit_bytes=_vmem_limit(block_bytes, 0, 4 * PREP_COLS * D_MODEL * 4)),
        name="prep_in_weights",
    )(src, w_t)


def _prep_out_kernel(wa_ref, wb_ref, wc_ref, wo_ref, oa_ref, ob_ref, oc_ref, oo_ref):
    oa_ref[...] = wa_ref[...].astype(BF16)
    ob_ref[...] = wb_ref[...].astype(BF16)
    oc_ref[...] = wc_ref[...].astype(BF16)
    oo_ref[...] = (0.5 * wo_ref[...]).astype(BF16)


def _prep_out_weights(w_out_a, w_out_b, w_out_c, w_o):
    steps = D_MODEL // PREP_ROWS
    rows = (CONV_WIDTH, HGRN_WIDTH, GDN_V_WIDTH, D_MODEL)
    slab = lambda r: pl.BlockSpec((None, r // steps, D_MODEL), lambda l, i: (l, i, 0))
    return pl.pallas_call(
        _prep_out_kernel,
        grid=(DEPTH, steps),
        in_specs=[slab(r) for r in rows],
        out_specs=[slab(r) for r in rows],
        out_shape=[jax.ShapeDtypeStruct((DEPTH, r, D_MODEL), BF16) for r in rows],
        compiler_params=pltpu.CompilerParams(
            dimension_semantics=("arbitrary", "arbitrary"),
            vmem_limit_bytes=_vmem_limit(sum(rows) // steps * D_MODEL * 6, 0, 0)),
        name="prep_out_weights",
    )(w_out_a, w_out_b, w_out_c, w_o)


def kernel(x, norm_w, w_in, b_gate, conv_a, conv_c, a_log, dt_bias, lower_bounds, hgrn_norm_w, gdn_norm_w,
           w_out_a, w_out_b, w_out_c, w_o, final_norm_w):
    bsz, seq, _ = x.shape
    x2 = x.reshape(bsz * seq, D_MODEL)
    w_b, w_c, w_m = _prep_in_weights(w_in)
    wa, wb, wc, wo = _prep_out_weights(w_out_a, w_out_b, w_out_c, w_o)
    pad8 = lambda a: jnp.pad(a.astype(F32), ((0, 0), (GDN_V_HEADS, SMALL_COLS - 2 * GDN_V_HEADS)))
    alog_pad, dtb_pad = pad8(a_log), pad8(dt_bias)
    final_w = final_norm_w[None, :]
    for l in range(DEPTH):
        yc = _gdn(x2, norm_w, w_c, conv_c, alog_pad, dtb_pad, gdn_norm_w, layer=l, bsz=bsz, seq=seq)
        x2 = _hgrn_merge(x2, norm_w, w_b, lower_bounds, hgrn_norm_w, w_m, yc, conv_a, b_gate, wa, wb, wc, wo,
                         final_w, layer=l, final=(l == DEPTH - 1), bsz=bsz, seq=seq)
    return x2.reshape(bsz, seq, D_MODEL)
```
